```python
import jax
import jax.numpy as jnp
from jax import lax
import numpy as np

D_MODEL = 1024
BATCH = 4
SEQ = 8192
DEPTH = 1

MIX_WIDTH = D_MODEL
HGRN_WIDTH = MIX_WIDTH // 2
HGRN_HEAD_DIM = 128
HGRN_HEADS = HGRN_WIDTH // HGRN_HEAD_DIM
HGRN_CHUNK = 64
GMLP_WIDTH = MIX_WIDTH - HGRN_WIDTH
GMLP_GROUP_DIM = 128
GMLP_GROUPS = GMLP_WIDTH // GMLP_GROUP_DIM
GMLP_CHUNK = 128
IN_COLS = 4 * HGRN_WIDTH + 2 * GMLP_WIDTH
PEER_HEADS = 8
PEER_N_KEYS = 128
PEER_N_EXPERTS = PEER_N_KEYS * PEER_N_KEYS
PEER_TOPK = 16
PEER_QUERY_DIM = 256
PEER_HALF = PEER_QUERY_DIM // 2
PEER_TOKEN_BLOCK = 128
NORM_EPS = 1e-6

kernel_name = 'hymba_hgrn2_gmlp_peer_adaln_layer'


def rmsnorm(x, g):
    xf = x.astype(jnp.float32)
    y = xf * lax.rsqrt(jnp.mean(xf * xf, axis=-1, keepdims=True) + NORM_EPS)
    return (y * g.astype(jnp.float32)).astype(x.dtype)


def layernorm(x, g, b):
    xf = x.astype(jnp.float32)
    mu = jnp.mean(xf, axis=-1, keepdims=True)
    var = jnp.mean(jnp.square(xf - mu), axis=-1, keepdims=True)
    y = (xf - mu) * lax.rsqrt(var + NORM_EPS)
    return (y * g.astype(jnp.float32) + b.astype(jnp.float32)).astype(x.dtype)


def hgrn2_chunkwise(q, k, v, logf):
    b_, s_, h_, dk = q.shape
    dv = v.shape[-1]
    nc = s_ // HGRN_CHUNK

    def to_chunks(t):
        return t.reshape(b_, nc, HGRN_CHUNK, h_, t.shape[-1]).transpose(1, 0, 3, 2, 4)

    qc, kc, vc = to_chunks(q), to_chunks(k), to_chunks(v)
    bc = jnp.cumsum(to_chunks(logf), axis=-2)
    causal = jnp.tril(jnp.ones((HGRN_CHUNK, HGRN_CHUNK), dtype=bool))[:, :, None]

    def step(state, inp):
        q_c, k_c, v_c, b_c = inp
        diff = b_c[..., :, None, :] - b_c[..., None, :, :]
        decay = jnp.exp(jnp.where(causal, diff, -jnp.inf))
        scores = jnp.einsum('bhtk,bhtsk,bhsk->bhts', q_c, decay, k_c)
        intra = jnp.einsum('bhts,bhsv->bhtv', scores, v_c)
        inter = jnp.einsum('bhtk,bhkv->bhtv', q_c * jnp.exp(b_c), state)
        b_last = b_c[..., -1:, :]
        new_state = (jnp.exp(b_last[..., 0, :])[..., None] * state
                     + jnp.einsum('bhsk,bhsv->bhkv', k_c * jnp.exp(b_last - b_c), v_c))
        return new_state, intra + inter

    state0 = jnp.zeros((b_, h_, dk, dv), jnp.float32)
    _, oc = lax.scan(step, state0, (qc, kc, vc, bc))
    return oc.transpose(1, 0, 3, 2, 4).reshape(b_, s_, h_, dv)


def hybrid_mixer(h, w_in, lb, hgrn_norm_g, gmlp_ln_g, gmlp_ln_b, spatial_w, spatial_b, gmlp_norm_g, w_out):
    b_, s_, _ = h.shape
    f32 = jnp.float32
    proj = h @ w_in
    cuts = [HGRN_WIDTH, 2 * HGRN_WIDTH, 3 * HGRN_WIDTH, 4 * HGRN_WIDTH, 4 * HGRN_WIDTH + GMLP_WIDTH]
    q, fz, inp, og, u, v = jnp.split(proj, cuts, axis=-1)

    heads = lambda t: t.reshape(b_, s_, HGRN_HEADS, HGRN_HEAD_DIM)
    fz32 = fz.astype(f32)
    logf = jnp.log(lb + (1.0 - lb) * jax.nn.sigmoid(fz32))
    key_gate = (1.0 - lb) * jax.nn.sigmoid(-fz32)
    q_feat = jax.nn.silu(q.astype(f32))
    o = hgrn2_chunkwise(heads(q_feat), heads(key_gate), heads(inp.astype(f32)), heads(logf))
    o = rmsnorm(o, hgrn_norm_g.reshape(HGRN_HEADS, HGRN_HEAD_DIM))
    o = o * heads(jax.nn.silu(og.astype(f32)))
    o = o.reshape(b_, s_, HGRN_WIDTH).astype(h.dtype)

    u = jax.nn.gelu(u)
    v = layernorm(jax.nn.gelu(v), gmlp_ln_g, gmlp_ln_b)
    nch = s_ // GMLP_CHUNK
    vc = v.reshape(b_, nch, GMLP_CHUNK, GMLP_GROUPS, GMLP_GROUP_DIM)
    tri = jnp.tril(jnp.ones((GMLP_CHUNK, GMLP_CHUNK), dtype=bool))[None]
    w_mask = jnp.where(tri, spatial_w, 0)
    mixed = jnp.einsum('gts,bnsgc->bntgc', w_mask, vc) + spatial_b.T[None, None, :, :, None]
    gm = u.reshape(b_, nch, GMLP_CHUNK, GMLP_GROUPS, GMLP_GROUP_DIM) * mixed
    gm = rmsnorm(gm.reshape(b_, s_, GMLP_GROUPS, GMLP_GROUP_DIM), gmlp_norm_g.reshape(GMLP_GROUPS, GMLP_GROUP_DIM))
    gm = gm.reshape(b_, s_, GMLP_WIDTH).astype(h.dtype)

    return jnp.concatenate([o, gm], axis=-1) @ w_out


def peer_ffn(h, wq, keys, u_tab, v_tab):
    b_, s_, d_ = h.shape
    t_ = b_ * s_
    hf = h.reshape(t_, d_)
    q = (hf @ wq).reshape(t_, PEER_HEADS, 2, PEER_HALF)
    sim = jnp.einsum('thpd,hpnd->thpn', q, keys)
    s1, i1 = lax.top_k(sim[:, :, 0, :], PEER_TOPK)
    s2, i2 = lax.top_k(sim[:, :, 1, :], PEER_TOPK)
    cand_s = (s1[..., :, None] + s2[..., None, :]).reshape(t_, PEER_HEADS, PEER_TOPK * PEER_TOPK)
    cand_i = (i1[..., :, None] * PEER_N_KEYS + i2[..., None, :]).reshape(t_, PEER_HEADS, PEER_TOPK * PEER_TOPK)
    top_s, pos = lax.top_k(cand_s, PEER_TOPK)
    idx = jnp.take_along_axis(cand_i, pos, axis=-1)
    gate = jax.nn.softmax(top_s.astype(jnp.float32), axis=-1).astype(h.dtype)
    nb = t_ // PEER_TOKEN_BLOCK
    idx_b = idx.reshape(nb, PEER_TOKEN_BLOCK, PEER_HEADS * PEER_TOPK)
    gate_b = gate.reshape(nb, PEER_TOKEN_BLOCK, PEER_HEADS * PEER_TOPK)
    h_b = hf.reshape(nb, PEER_TOKEN_BLOCK, d_)

    def block(args):
        hb, ib, gb = args
        u_sel = jnp.take(u_tab, ib, axis=0)
        act = jax.nn.gelu(jnp.einsum('tkd,td->tk', u_sel, hb)) * gb
        return jnp.einsum('tk,tkd->td', act, jnp.take(v_tab, ib, axis=0))

    y = lax.map(block, (h_b, idx_b, gate_b))
    return y.reshape(b_, s_, d_)


def setup_inputs(seed: int = 0) -> dict:
    key = jax.random.key(seed)
    ks = jax.random.split(key, 20)
    f32 = jnp.float32

    def nrm(k, shape, scale):
        return jax.random.normal(k, shape, f32) * scale

    return {
        'x': nrm(ks[0], (BATCH, SEQ, D_MODEL), 1.0),
        'c': nrm(ks[1], (BATCH, D_MODEL), 1.0),
        'ada_w': nrm(ks[2], (DEPTH, D_MODEL, 6 * D_MODEL), 0.5 * D_MODEL ** -0.5),
        'ada_b': nrm(ks[3], (DEPTH, 6 * D_MODEL), 0.01),
        'norm1_g': 1.0 + nrm(ks[4], (DEPTH, D_MODEL), 0.02),
        'w_in': nrm(ks[5], (DEPTH, D_MODEL, IN_COLS), D_MODEL ** -0.5),
        'lb_gamma': nrm(ks[6], (DEPTH + 1, HGRN_WIDTH), 0.1),
        'hgrn_norm_g': 1.0 + nrm(ks[7], (DEPTH, HGRN_WIDTH), 0.02),
        'gmlp_ln_g': 1.0 + nrm(ks[8], (DEPTH, GMLP_WIDTH), 0.02),
        'gmlp_ln_b': nrm(ks[9], (DEPTH, GMLP_WIDTH), 0.02),
        'spatial_w': nrm(ks[10], (DEPTH, GMLP_GROUPS, GMLP_CHUNK, GMLP_CHUNK), GMLP_CHUNK ** -0.5),
        'spatial_b': 1.0 + nrm(ks[11], (DEPTH, GMLP_GROUPS, GMLP_CHUNK), 0.02),
        'gmlp_norm_g': 1.0 + nrm(ks[12], (DEPTH, GMLP_WIDTH), 0.02),
        'w_out': nrm(ks[13], (DEPTH, MIX_WIDTH, D_MODEL), MIX_WIDTH ** -0.5),
        'norm2_g': 1.0 + nrm(ks[14], (DEPTH, D_MODEL), 0.02),
        'peer_wq': nrm(ks[15], (DEPTH, D_MODEL, PEER_HEADS * PEER_QUERY_DIM), D_MODEL ** -0.5),
        'peer_keys': nrm(ks[16], (DEPTH, PEER_HEADS, 2, PEER_N_KEYS, PEER_HALF), PEER_HALF ** -0.5),
        'peer_u': nrm(ks[17], (DEPTH, PEER_N_EXPERTS, D_MODEL), D_MODEL ** -0.5),
        'peer_v': nrm(ks[18], (DEPTH, PEER_N_EXPERTS, D_MODEL), 0.5),
        'final_g': 1.0 + nrm(ks[19], (D_MODEL,), 0.02),
    }


def reference(x, c, ada_w, ada_b, norm1_g, w_in, lb_gamma, hgrn_norm_g, gmlp_ln_g, gmlp_ln_b,
              spatial_w, spatial_b, gmlp_norm_g, w_out, norm2_g, peer_wq, peer_keys, peer_u, peer_v, final_g):
    lower_bounds = jnp.cumsum(jax.nn.softmax(lb_gamma.astype(jnp.float32), axis=0), axis=0)
    c_act = jax.nn.silu(c)
    for l in range(DEPTH):
        mod = c_act @ ada_w[l] + ada_b[l]
        shift1, scale1, gate1, shift2, scale2, gate2 = [m[:, None, :] for m in jnp.split(mod, 6, axis=-1)]
        h = rmsnorm(x, norm1_g[l]) * (1 + scale1) + shift1
        x = x + gate1 * hybrid_mixer(h, w_in[l], lower_bounds[l], hgrn_norm_g[l], gmlp_ln_g[l], gmlp_ln_b[l],
                                     spatial_w[l], spatial_b[l], gmlp_norm_g[l], w_out[l])
        h = rmsnorm(x, norm2_g[l]) * (1 + scale2) + shift2
        x = x + gate2 * peer_ffn(h, peer_wq[l], peer_keys[l], peer_u[l], peer_v[l])
    return rmsnorm(x, final_g)
```

```python
import functools

import jax
import jax.numpy as jnp
from jax import lax
from jax.experimental import pallas as pl
from jax.experimental.pallas import tpu as pltpu

F32 = jnp.float32
BF16 = jnp.bfloat16
I32 = jnp.int32
HIGHEST = lax.Precision.HIGHEST
NORM_EPS = 1e-6

LANES = 128
SUBLANES = 8
HEAD_DIM = 128
GMLP_CHUNK = 128
SUB = 16
PEER_TOPK = 16
N_KEYS = 128
VMEM_LIMIT_MIX = 48 * 1024 * 1024
VMEM_LIMIT_PEER = 52 * 1024 * 1024


def _gelu(x):
    return 0.5 * x * (1.0 + jnp.tanh(0.7978845608028654 * (x + 0.044715 * (x * x * x))))


def _sigmoid(x):
    return 1.0 / (1.0 + jnp.exp(-x))


def _rms(x, eps=NORM_EPS):
    return x * lax.rsqrt(jnp.mean(x * x, axis=-1, keepdims=True) + eps)


def _const_spec(shape):
    nd = len(shape)
    return pl.BlockSpec(shape, lambda *_: (0,) * nd)


def _mod_kernel(c_ref, w_ref, b_ref, o_ref):
    c = c_ref[...]
    ca = c * _sigmoid(c)
    o_ref[...] = jnp.dot(ca, w_ref[...], precision=HIGHEST, preferred_element_type=F32) + b_ref[...]


def _mod_call(c_pad, w, b):
    rows, d = c_pad.shape
    n = w.shape[1]
    bn = 1536 if n % 1536 == 0 else n
    return pl.pallas_call(
        _mod_kernel,
        grid=(n // bn,),
        in_specs=[pl.BlockSpec((rows, d), lambda j: (0, 0)),
                  pl.BlockSpec((d, bn), lambda j: (0, j)),
                  pl.BlockSpec((1, bn), lambda j: (0, j))],
        out_specs=pl.BlockSpec((rows, bn), lambda j: (0, j)),
        out_shape=jax.ShapeDtypeStruct((rows, n), F32),
        name="adaln_mod",
    )(c_pad, w, b.reshape(1, n))


def _mix_kernel(x_ref, mod_ref, g1_ref, win_ref, lb_ref, hg_ref, lng_ref, lnb_ref, gng_ref,
                sw_ref, sbt_ref, wout_ref, o_ref,
                state, qe_s, ke_s, qf_s, kg_s, bb_s, vv_s, cat_s, *, tb, hw, gw):
    n_heads = hw // HEAD_DIM
    n_groups = gw // HEAD_DIM

    @pl.when(pl.program_id(1) == 0)
    def _():
        state[...] = jnp.zeros_like(state)

    x = x_ref[...]
    shift, scale, gate = mod_ref[0:1, :], mod_ref[1:2, :], mod_ref[2:3, :]
    h = (_rms(x) * g1_ref[...]) * (1.0 + scale) + shift
    proj = jnp.dot(h.astype(BF16), win_ref[...], preferred_element_type=F32)

    lb = lb_ref[...]
    q = proj[:, 0:hw]
    fz = proj[:, hw:2 * hw]
    e = jnp.exp(-jnp.abs(fz))
    r = 1.0 / (1.0 + e)
    pos = fz >= 0.0
    sig = jnp.where(pos, r, e * r)
    sig_neg = jnp.where(pos, e * r, r)
    logf = jnp.log(lb + (1.0 - lb) * sig)
    kg = (1.0 - lb) * sig_neg
    qf = q * _sigmoid(q)
    ri = lax.broadcasted_iota(I32, (tb, tb), 0)
    ci = lax.broadcasted_iota(I32, (tb, tb), 1)
    same = (ri // SUB) == (ci // SUB)
    low = jnp.where(same & (ci <= ri), 1.0, 0.0).astype(F32)
    upp = jnp.where(same & (ci > ri), 1.0, 0.0).astype(F32)
    bb = jnp.dot(low, logf, precision=HIGHEST, preferred_element_type=F32)
    rem = jnp.dot(upp, logf, precision=HIGHEST, preferred_element_type=F32)
    qe_s[...] = qf * jnp.exp(bb)
    ke_s[...] = kg * jnp.exp(rem)
    qf_s[...] = qf
    kg_s[...] = kg
    bb_s[...] = bb
    vv_s[...] = proj[:, 2 * hw:3 * hw]

    rows16 = lax.broadcasted_iota(I32, (SUB, HEAD_DIM), 0)
    rows128 = lax.broadcasted_iota(I32, (GMLP_CHUNK, HEAD_DIM), 0)
    steps_per_chunk = GMLP_CHUNK // SUB

    def chunk_body(c, carry):
        c0 = pl.multiple_of(c * GMLP_CHUNK, GMLP_CHUNK)
        for hd in range(n_heads):
            cs = slice(hd * HEAD_DIM, (hd + 1) * HEAD_DIM)
            qe_c = qe_s[pl.ds(c0, GMLP_CHUNK), cs]
            ke_c = ke_s[pl.ds(c0, GMLP_CHUNK), cs]
            qf_c = qf_s[pl.ds(c0, GMLP_CHUNK), cs]
            kg_c = kg_s[pl.ds(c0, GMLP_CHUNK), cs]
            bb_c = bb_s[pl.ds(c0, GMLP_CHUNK), cs]
            v_c = vv_s[pl.ds(c0, GMLP_CHUNK), cs]
            v_t = v_c.T.astype(BF16)
            st = state[hd]
            outs = []
            for j in range(steps_per_chunk):
                rs = slice(j * SUB, (j + 1) * SUB)
                bb_b, qf_b, kg_b, v_b = bb_c[rs], qf_c[rs], kg_c[rs], v_c[rs]
                o = lax.dot_general(qe_c[rs].astype(BF16), st.astype(BF16), (((1,), (1,)), ((), ())),
                                    preferred_element_type=F32)
                for s in range(SUB):
                    dec = jnp.where(rows16 >= s, jnp.exp(bb_b - bb_b[s:s + 1, :]), 0.0)
                    w = jnp.sum(qf_b * dec * kg_b[s:s + 1, :], axis=-1, keepdims=True)
                    o = o + w * v_b[s:s + 1, :]
                outs.append(o)
                in_step = (rows128 >= j * SUB) & (rows128 < (j + 1) * SUB)
                ke_m = jnp.where(in_step, ke_c, 0.0).astype(BF16)
                upd = jnp.dot(v_t, ke_m, preferred_element_type=F32)
                st = jnp.exp(bb_b[SUB - 1:SUB, :]) * st + upd
            state[hd] = st
            o_c = jnp.concatenate(outs, axis=0)
            cat_s[pl.ds(c0, GMLP_CHUNK), cs] = _rms(o_c) * hg_ref[:, cs]
        return carry

    lax.fori_loop(0, tb // GMLP_CHUNK, chunk_body, 0)

    og = proj[:, 3 * hw:4 * hw]
    o_all = cat_s[:, 0:hw] * (og * _sigmoid(og))

    u = _gelu(proj[:, 4 * hw:4 * hw + gw])
    v = _gelu(proj[:, 4 * hw + gw:4 * hw + 2 * gw])
    mu = jnp.mean(v, axis=-1, keepdims=True)
    vc = v - mu
    var = jnp.mean(vc * vc, axis=-1, keepdims=True)
    vln = (vc * lax.rsqrt(var + NORM_EPS)) * lng_ref[...] + lnb_ref[...]
    tri = (lax.broadcasted_iota(I32, (GMLP_CHUNK, GMLP_CHUNK), 1)
           <= lax.broadcasted_iota(I32, (GMLP_CHUNK, GMLP_CHUNK), 0))
    gm_groups = []
    for g in range(n_groups):
        gs = slice(g * HEAD_DIM, (g + 1) * HEAD_DIM)
        wm = jnp.where(tri, sw_ref[g], 0.0).astype(BF16)
        bias = sbt_ref[:, g:g + 1]
        parts = []
        for cidx in range(tb // GMLP_CHUNK):
            rs = slice(cidx * GMLP_CHUNK, (cidx + 1) * GMLP_CHUNK)
            mixed = jnp.dot(wm, vln[rs, gs].astype(BF16), preferred_element_type=F32) + bias
            parts.append(u[rs, gs] * mixed)
        gmg = jnp.concatenate(parts, axis=0) if len(parts) > 1 else parts[0]
        gm_groups.append(_rms(gmg) * gng_ref[:, gs])
    gm = jnp.concatenate(gm_groups, axis=-1)

    cat = jnp.concatenate([o_all, gm], axis=-1).astype(BF16)
    mixed_out = jnp.dot(cat, wout_ref[...], preferred_element_type=F32)
    o_ref[...] = x + gate * mixed_out


def _mix_call(x, mod8, g1, w_in, lb, hg, lng, lnb, gng, sw, sbt, w_out, *, tb):
    b, s, d = x.shape
    hw = lb.shape[1]
    gw = lng.shape[1]
    n_heads = hw // HEAD_DIM
    kern = functools.partial(_mix_kernel, tb=tb, hw=hw, gw=gw)
    return pl.pallas_call(
        kern,
        grid=(b, s // tb),
        in_specs=[pl.BlockSpec((None, tb, d), lambda i, j: (i, j, 0)),
                  pl.BlockSpec((None, 8, d), lambda i, j: (i, 0, 0)),
                  _const_spec(g1.shape), _const_spec(w_in.shape), _const_spec(lb.shape),
                  _const_spec(hg.shape), _const_spec(lng.shape), _const_spec(lnb.shape),
                  _const_spec(gng.shape), _const_spec(sw.shape), _const_spec(sbt.shape),
                  _const_spec(w_out.shape)],
        out_specs=pl.BlockSpec((None, tb, d), lambda i, j: (i, j, 0)),
        out_shape=jax.ShapeDtypeStruct((b, s, d), F32),
        scratch_shapes=[pltpu.VMEM((n_heads, HEAD_DIM, HEAD_DIM), F32)]
                       + [pltpu.VMEM((tb, hw), F32) for _ in range(7)],
        compiler_params=pltpu.CompilerParams(dimension_semantics=("arbitrary", "arbitrary"),
                                             vmem_limit_bytes=VMEM_LIMIT_MIX),
        name="hgrn_gmlp_mixer",
    )(x, mod8, g1, w_in, lb, hg, lng, lnb, gng, sw, sbt, w_out)


def _topk_rows(x, k):
    n = x.shape[0]
    rid = lax.broadcasted_iota(I32, x.shape, 0)
    vals, ids = [], []
    for _ in range(k):
        m = jnp.max(x, axis=0, keepdims=True)
        am = jnp.min(jnp.where(x == m, rid, n), axis=0, keepdims=True)
        vals.append(m)
        ids.append(am)
        x = jnp.where(rid == am, -jnp.inf, x)
    return jnp.concatenate(vals, axis=0), jnp.concatenate(ids, axis=0)


def _route_kernel(x_ref, mod_ref, g2_ref, wq_ref, keys_ref, h_ref, idx_ref, gate_ref, *, n_heads):
    x = x_ref[...]
    shift, scale = mod_ref[3:4, :], mod_ref[4:5, :]
    h = (_rms(x) * g2_ref[...]) * (1.0 + scale) + shift
    h_ref[...] = h
    q = jnp.dot(h.astype(BF16), wq_ref[...], preferred_element_type=F32)
    tb = x.shape[0]
    k = PEER_TOPK
    jrow8 = lax.broadcasted_iota(I32, (SUBLANES, tb), 0)
    idx_rows, gate_rows = [], []
    for hh in range(n_heads):
        tops = []
        for p in range(2):
            col = (hh * 2 + p) * HEAD_DIM
            sim_t = lax.dot_general(keys_ref[hh, p].astype(BF16), q[:, col:col + HEAD_DIM].astype(BF16),
                                    (((1,), (1,)), ((), ())), preferred_element_type=F32)
            tops.append(_topk_rows(sim_t, k))
        (s1, i1), (s2, i2) = tops
        cs = [s1[0:1] + s2]
        ci = [i1[0:1] * N_KEYS + i2]
        for i in range(1, SUBLANES):
            nvalid = k // (i + 1)
            cs.append(jnp.where(jrow8 < nvalid, s1[i:i + 1] + s2[0:SUBLANES], -jnp.inf))
            ci.append(i1[i:i + 1] * N_KEYS + i2[0:SUBLANES])
        cs.append(s1[SUBLANES:k] + s2[0:1])
        ci.append(i1[SUBLANES:k] * N_KEYS + i2[0:1])
        cand_s = jnp.concatenate(cs, axis=0)
        cand_i = jnp.concatenate(ci, axis=0)
        top_s, pos = _topk_rows(cand_s, k)
        prow = lax.broadcasted_iota(I32, cand_s.shape, 0)
        sel = [jnp.sum(jnp.where(prow == pos[r:r + 1], cand_i, 0), axis=0, keepdims=True) for r in range(k)]
        top_i = jnp.concatenate(sel, axis=0)
        ex = jnp.exp(top_s - top_s[0:1])
        gate_rows.append(ex / jnp.sum(ex, axis=0, keepdims=True))
        idx_rows.append(top_i)
    idx_t = jnp.concatenate(idx_rows, axis=0)
    gate_t = jnp.concatenate(gate_rows, axis=0)
    idx_ref[...] = (idx_t * 4).T
    gate_ref[...] = gate_t.T


def _route_call(x1, mod8, g2, wq, keys, *, tb):
    b, s, d = x1.shape
    n_heads = keys.shape[0]
    nsel = n_heads * PEER_TOPK
    nblk = s // tb
    kern = functools.partial(_route_kernel, n_heads=n_heads)
    return pl.pallas_call(
        kern,
        grid=(b, nblk),
        in_specs=[pl.BlockSpec((None, tb, d), lambda i, j: (i, j, 0)),
                  pl.BlockSpec((None, 8, d), lambda i, j: (i, 0, 0)),
                  _const_spec(g2.shape), _const_spec(wq.shape), _const_spec(keys.shape)],
        out_specs=[pl.BlockSpec((tb, d), lambda i, j: (i * nblk + j, 0)),
                   pl.BlockSpec((tb, nsel), lambda i, j: (i * nblk + j, 0)),
                   pl.BlockSpec((tb, nsel), lambda i, j: (i * nblk + j, 0))],
        out_shape=[jax.ShapeDtypeStruct((b * s, d), F32),
                   jax.ShapeDtypeStruct((b * s, nsel), I32),
                   jax.ShapeDtypeStruct((b * s, nsel), F32)],
        compiler_params=pltpu.CompilerParams(dimension_semantics=("arbitrary", "arbitrary"),
                                             vmem_limit_bytes=VMEM_LIMIT_MIX),
        name="peer_route",
    )(x1, mod8, g2, wq, keys)


def _pack_table(tab):
    n, d = tab.shape
    p = d // (2 * LANES)
    tb = tab.astype(BF16).reshape(n, p, 2, LANES).transpose(0, 1, 3, 2)
    return lax.bitcast_convert_type(tb, I32).reshape(n * p, LANES)


def _gather_row(tab_ref, idx_ref, pos):
    off = pl.multiple_of(idx_ref[pos], 4)
    return pltpu.bitcast(tab_ref[pl.ds(off, 4), :], BF16).astype(F32)


def _peer_u_kernel(idx_ref, h_ref, gate_ref, tab_ref, act_ref, *, tbp, nsel):
    sub = lax.broadcasted_iota(I32, (SUBLANES, LANES), 0)
    m4 = (sub & 4) == 0
    m2 = (sub & 2) == 0
    m1 = (sub & 1) == 0

    def fold(a, b, dist, mask):
        return jnp.where(mask, a + pltpu.roll(a, SUBLANES - dist, 0), b + pltpu.roll(b, dist, 0))

    def body(t, carry):
        base = t * nsel
        hv = h_ref[pl.ds(pl.multiple_of(t * SUBLANES, SUBLANES), SUBLANES), :]
        rows = []
        for g in range(nsel // SUBLANES):
            p = [_gather_row(tab_ref, idx_ref, base + g * SUBLANES + j) * hv for j in range(SUBLANES)]
            a0, a1, a2, a3 = fold(p[0], p[4], 4, m4), fold(p[2], p[6], 4, m4), fold(p[1], p[5], 4, m4), fold(p[3], p[7], 4, m4)
            b0, b1 = fold(a0, a1, 2, m2), fold(a2, a3, 2, m2)
            rows.append(fold(b0, b1, 1, m1))
        emat = jnp.concatenate(rows, axis=0)
        s = jnp.sum(emat.T, axis=0, keepdims=True)
        act_ref[pl.ds(t, 1), :] = _gelu(s) * gate_ref[pl.ds(t, 1), :]
        return carry

    lax.fori_loop(0, tbp, body, 0)


def _peer_v_kernel(idx_ref, act_ref, tab_ref, y_ref, a2_s, *, tbp, nsel):
    def body(t, carry):
        base = t * nsel
        arow = act_ref[pl.ds(t, 1), :]
        a2_s[...] = jnp.broadcast_to(arow, (nsel, nsel)).T
        accs = [jnp.zeros((SUBLANES, LANES), F32) for _ in range(4)]
        for k in range(nsel):
            w = _gather_row(tab_ref, idx_ref, base + k)
            a = jnp.broadcast_to(a2_s[k:k + 1, :], (SUBLANES, LANES))
            accs[k % 4] = accs[k % 4] + a * w
        y_ref[pl.ds(pl.multiple_of(t * SUBLANES, SUBLANES), SUBLANES), :] = (accs[0] + accs[1]) + (accs[2] + accs[3])
        return carry

    lax.fori_loop(0, tbp, body, 0)


def _table_spec(shape):
    return pl.BlockSpec(shape, lambda i: (0, 0), pipeline_mode=pl.Buffered(1))


def _peer_u_call(idx_flat, h8, gate, tab, *, tbp):
    t, nsel = gate.shape
    kern = functools.partial(_peer_u_kernel, tbp=tbp, nsel=nsel)
    return pl.pallas_call(
        kern,
        grid=(t // tbp,),
        in_specs=[pl.BlockSpec((tbp * nsel,), lambda i: (i,), memory_space=pltpu.SMEM),
                  pl.BlockSpec((tbp * SUBLANES, LANES), lambda i: (i, 0)),
                  pl.BlockSpec((tbp, nsel), lambda i: (i, 0)),
                  _table_spec(tab.shape)],
        out_specs=pl.BlockSpec((tbp, nsel), lambda i: (i, 0)),
        out_shape=jax.ShapeDtypeStruct((t, nsel), F32),
        compiler_params=pltpu.CompilerParams(dimension_semantics=("arbitrary",),
                                             vmem_limit_bytes=VMEM_LIMIT_PEER),
        name="peer_expert_scores",
    )(idx_flat, h8, gate, tab)


def _peer_v_call(idx_flat, act, tab, *, tbp):
    t, nsel = act.shape
    kern = functools.partial(_peer_v_kernel, tbp=tbp, nsel=nsel)
    return pl.pallas_call(
        kern,
        grid=(t // tbp,),
        in_specs=[pl.BlockSpec((tbp * nsel,), lambda i: (i,), memory_space=pltpu.SMEM),
                  pl.BlockSpec((tbp, nsel), lambda i: (i, 0)),
                  _table_spec(tab.shape)],
        out_specs=pl.BlockSpec((tbp * SUBLANES, LANES), lambda i: (i, 0)),
        out_shape=jax.ShapeDtypeStruct((t * SUBLANES, LANES), F32),
        scratch_shapes=[pltpu.VMEM((nsel, nsel), F32)],
        compiler_params=pltpu.CompilerParams(dimension_semantics=("arbitrary",),
                                             vmem_limit_bytes=VMEM_LIMIT_PEER),
        name="peer_expert_mix",
    )(idx_flat, act, tab)


def _resid_kernel(x_ref, y_ref, mod_ref, g_ref, o_ref, *, final):
    z = x_ref[...] + mod_ref[5:6, :] * y_ref[...]
    o_ref[...] = _rms(z) * g_ref[...] if final else z


def _resid_call(x1, y, mod8, g, *, tb, final):
    b, s, d = x1.shape
    nblk = s // tb
    return pl.pallas_call(
        functools.partial(_resid_kernel, final=final),
        grid=(b, nblk),
        in_specs=[pl.BlockSpec((None, tb, d), lambda i, j: (i, j, 0)),
                  pl.BlockSpec((tb, d), lambda i, j: (i * nblk + j, 0)),
                  pl.BlockSpec((None, 8, d), lambda i, j: (i, 0, 0)),
                  _const_spec(g.shape)],
        out_specs=pl.BlockSpec((None, tb, d), lambda i, j: (i, j, 0)),
        out_shape=jax.ShapeDtypeStruct((b, s, d), F32),
        name="peer_residual_norm",
    )(x1, y, mod8, g)


def _pick_block(n, pref):
    while n % pref:
        pref //= 2
    return pref


def kernel(x, c, ada_w, ada_b, norm1_g, w_in, lb_gamma, hgrn_norm_g, gmlp_ln_g, gmlp_ln_b, spatial_w,
           spatial_b, gmlp_norm_g, w_out, norm2_g, peer_wq, peer_keys, peer_u, peer_v, final_g):
    b, s, d = x.shape
    depth = ada_w.shape[0]
    t = b * s
    lower_bounds = jnp.cumsum(jax.nn.softmax(lb_gamma.astype(F32), axis=0), axis=0)
    c_pad = jnp.zeros((8, d), F32).at[:b].set(c)
    tb_mix = _pick_block(s, 256)
    tb_route = _pick_block(s, 256)
    tb_res = _pick_block(s, 512)
    tbp = _pick_block(t, 32)
    for l in range(depth):
        mod = _mod_call(c_pad, ada_w[l], ada_b[l])[:b]
        mod8 = jnp.concatenate([mod.reshape(b, 6, d), jnp.zeros((b, 2, d), F32)], axis=1)
        x1 = _mix_call(x, mod8, norm1_g[l][None], w_in[l].astype(BF16), lower_bounds[l][None],
                       hgrn_norm_g[l][None], gmlp_ln_g[l][None], gmlp_ln_b[l][None], gmlp_norm_g[l][None],
                       spatial_w[l], spatial_b[l].T, w_out[l].astype(BF16), tb=tb_mix)
        h2, idx4, gate = _route_call(x1, mod8, norm2_g[l][None], peer_wq[l].astype(BF16), peer_keys[l],
                                     tb=tb_route)
        idx_flat = idx4.reshape(-1)
        act = _peer_u_call(idx_flat, h2.reshape(t * SUBLANES, LANES), gate, _pack_table(peer_u[l]), tbp=tbp)
        y8 = _peer_v_call(idx_flat, act, _pack_table(peer_v[l]), tbp=tbp)
        last = l == depth - 1
        x = _resid_call(x1, y8.reshape(t, d), mod8, final_g[None] if last else jnp.ones((1, d), F32),
                        tb=tb_res, final=last)
    return x
```

```python
import functools

import jax
import jax.numpy as jnp
from jax import lax
from jax.experimental import pallas as pl
from jax.experimental.pallas import tpu as pltpu

F32 = jnp.float32
BF16 = jnp.bfloat16
I32 = jnp.int32
HIGHEST = lax.Precision.HIGHEST
NORM_EPS = 1e-6

LANES = 128
SUBLANES = 8
HEAD_DIM = 128
GMLP_CHUNK = 128
SUB = 16
PEER_TOPK = 16
N_KEYS = 128
VMEM_LIMIT_MIX = 48 * 1024 * 1024
VMEM_LIMIT_PEER = 52 * 1024 * 1024


def _gelu(x):
    return 0.5 * x * (1.0 + jnp.tanh(0.7978845608028654 * (x + 0.044715 * (x * x * x))))


def _sigmoid(x):
    return 1.0 / (1.0 + jnp.exp(-x))


def _rms(x, eps=NORM_EPS):
    return x * lax.rsqrt(jnp.mean(x * x, axis=-1, keepdims=True) + eps)


def _const_spec(shape):
    nd = len(shape)
    return pl.BlockSpec(shape, lambda *_: (0,) * nd)


def _mod_kernel(c_ref, w_ref, b_ref, o_ref):
    c = c_ref[...]
    ca = c * _sigmoid(c)
    o_ref[...] = jnp.dot(ca, w_ref[...], precision=HIGHEST, preferred_element_type=F32) + b_ref[...]


def _mod_call(c_pad, w, b):
    rows, d = c_pad.shape
    n = w.shape[1]
    bn = 1536 if n % 1536 == 0 else n
    return pl.pallas_call(
        _mod_kernel,
        grid=(n // bn,),
        in_specs=[pl.BlockSpec((rows, d), lambda j: (0, 0)),
                  pl.BlockSpec((d, bn), lambda j: (0, j)),
                  pl.BlockSpec((1, bn), lambda j: (0, j))],
        out_specs=pl.BlockSpec((rows, bn), lambda j: (0, j)),
        out_shape=jax.ShapeDtypeStruct((rows, n), F32),
        name="adaln_mod",
    )(c_pad, w, b.reshape(1, n))


def _mix_kernel(x_ref, mod_ref, g1_ref, win_ref, lb_ref, hg_ref, lng_ref, lnb_ref, gng_ref,
                sw_ref, sbt_ref, wout_ref, o_ref,
                state, qe_s, ke_s, qf_s, kg_s, bb_s, vv_s, cat_s, *, tb, hw, gw):
    n_heads = hw // HEAD_DIM
    n_groups = gw // HEAD_DIM

    @pl.when(pl.program_id(1) == 0)
    def _():
        state[...] = jnp.zeros_like(state)

    x = x_ref[...]
    shift, scale, gate = mod_ref[0:1, :], mod_ref[1:2, :], mod_ref[2:3, :]
    h = (_rms(x) * g1_ref[...]) * (1.0 + scale) + shift
    proj = jnp.dot(h.astype(BF16), win_ref[...], preferred_element_type=F32)

    lb = lb_ref[...]
    q = proj[:, 0:hw]
    fz = proj[:, hw:2 * hw]
    e = jnp.exp(-jnp.abs(fz))
    r = 1.0 / (1.0 + e)
    pos = fz >= 0.0
    sig = jnp.where(pos, r, e * r)
    sig_neg = jnp.where(pos, e * r, r)
    logf = jnp.log(lb + (1.0 - lb) * sig)
    kg = (1.0 - lb) * sig_neg
    qf = q * _sigmoid(q)
    ri = lax.broadcasted_iota(I32, (tb, tb), 0)
    ci = lax.broadcasted_iota(I32, (tb, tb), 1)
    same = (ri // SUB) == (ci // SUB)
    low = jnp.where(same & (ci <= ri), 1.0, 0.0).astype(F32)
    upp = jnp.where(same & (ci > ri), 1.0, 0.0).astype(F32)
    bb = jnp.dot(low, logf, precision=HIGHEST, preferred_element_type=F32)
    rem = jnp.dot(upp, logf, precision=HIGHEST, preferred_element_type=F32)
    qe_s[...] = qf * jnp.exp(bb)
    ke_s[...] = kg * jnp.exp(rem)
    qf_s[...] = qf
    kg_s[...] = kg
    bb_s[...] = bb
    vv_s[...] = proj[:, 2 * hw:3 * hw]

    rows16 = lax.broadcasted_iota(I32, (SUB, HEAD_DIM), 0)
    rows128 = lax.broadcasted_iota(I32, (GMLP_CHUNK, HEAD_DIM), 0)
    steps_per_chunk = GMLP_CHUNK // SUB

    def chunk_body(c, carry):
        c0 = pl.multiple_of(c * GMLP_CHUNK, GMLP_CHUNK)
        for hd in range(n_heads):
            cs = slice(hd * HEAD_DIM, (hd + 1) * HEAD_DIM)
            qe_c = qe_s[pl.ds(c0, GMLP_CHUNK), cs]
            ke_c = ke_s[pl.ds(c0, GMLP_CHUNK), cs]
            qf_c = qf_s[pl.ds(c0, GMLP_CHUNK), cs]
            kg_c = kg_s[pl.ds(c0, GMLP_CHUNK), cs]
            bb_c = bb_s[pl.ds(c0, GMLP_CHUNK), cs]
            v_c = vv_s[pl.ds(c0, GMLP_CHUNK), cs]
            v_t = v_c.T.astype(BF16)
            st = state[hd]
            outs = []
            for j in range(steps_per_chunk):
                rs = slice(j * SUB, (j + 1) * SUB)
                bb_b, qf_b, kg_b, v_b = bb_c[rs], qf_c[rs], kg_c[rs], v_c[rs]
                o = lax.dot_general(qe_c[rs].astype(BF16), st.astype(BF16), (((1,), (1,)), ((), ())),
                                    preferred_element_type=F32)
                for s in range(SUB):
                    dec = jnp.where(rows16 >= s, jnp.exp(bb_b - bb_b[s:s + 1, :]), 0.0)
                    w = jnp.sum(qf_b * dec * kg_b[s:s + 1, :], axis=-1, keepdims=True)
                    o = o + w * v_b[s:s + 1, :]
                outs.append(o)
                in_step = (rows128 >= j * SUB) & (rows128 < (j + 1) * SUB)
                ke_m = jnp.where(in_step, ke_c, 0.0).astype(BF16)
                upd = jnp.dot(v_t, ke_m, preferred_element_type=F32)
                st = jnp.exp(bb_b[SUB - 1:SUB, :]) * st + upd
            state[hd] = st
            o_c = jnp.concatenate(outs, axis=0)
            cat_s[pl.ds(c0, GMLP_CHUNK), cs] = _rms(o_c) * hg_ref[:, cs]
        return carry

    lax.fori_loop(0, tb // GMLP_CHUNK, chunk_body, 0)

    og = proj[:, 3 * hw:4 * hw]
    o_all = cat_s[:, 0:hw] * (og * _sigmoid(og))

    u = _gelu(proj[:, 4 * hw:4 * hw + gw])
    v = _gelu(proj[:, 4 * hw + gw:4 * hw + 2 * gw])
    mu = jnp.mean(v, axis=-1, keepdims=True)
    vc = v - mu
    var = jnp.mean(vc * vc, axis=-1, keepdims=True)
    vln = (vc * lax.rsqrt(var + NORM_EPS)) * lng_ref[...] + lnb_ref[...]
    tri = (lax.broadcasted_iota(I32, (GMLP_CHUNK, GMLP_CHUNK), 1)
           <= lax.broadcasted_iota(I32, (GMLP_CHUNK, GMLP_CHUNK), 0))
    gm_groups = []
    for g in range(n_groups):
        gs = slice(g * HEAD_DIM, (g + 1) * HEAD_DIM)
        wm = jnp.where(tri, sw_ref[g], 0.0).astype(BF16)
        bias = sbt_ref[:, g:g + 1]
        parts = []
        for cidx in range(tb // GMLP_CHUNK):
            rs = slice(cidx * GMLP_CHUNK, (cidx + 1) * GMLP_CHUNK)
            mixed = jnp.dot(wm, vln[rs, gs].astype(BF16), preferred_element_type=F32) + bias
            parts.append(u[rs, gs] * mixed)
        gmg = jnp.concatenate(parts, axis=0) if len(parts) > 1 else parts[0]
        gm_groups.append(_rms(gmg) * gng_ref[:, gs])
    gm = jnp.concatenate(gm_groups, axis=-1)

    cat = jnp.concatenate([o_all, gm], axis=-1).astype(BF16)
    mixed_out = jnp.dot(cat, wout_ref[...], preferred_element_type=F32)
    o_ref[...] = x + gate * mixed_out


def _mix_call(x, mod8, g1, w_in, lb, hg, lng, lnb, gng, sw, sbt, w_out, *, tb):
    b, s, d = x.shape
    hw = lb.shape[1]
    gw = lng.shape[1]
    n_heads = hw // HEAD_DIM
    kern = functools.partial(_mix_kernel, tb=tb, hw=hw, gw=gw)
    return pl.pallas_call(
        kern,
        grid=(b, s // tb),
        in_specs=[pl.BlockSpec((None, tb, d), lambda i, j: (i, j, 0)),
                  pl.BlockSpec((None, 8, d), lambda i, j: (i, 0, 0)),
                  _const_spec(g1.shape), _const_spec(w_in.shape), _const_spec(lb.shape),
                  _const_spec(hg.shape), _const_spec(lng.shape), _const_spec(lnb.shape),
                  _const_spec(gng.shape), _const_spec(sw.shape), _const_spec(sbt.shape),
                  _const_spec(w_out.shape)],
        out_specs=pl.BlockSpec((None, tb, d), lambda i, j: (i, j, 0)),
        out_shape=jax.ShapeDtypeStruct((b, s, d), F32),
        scratch_shapes=[pltpu.VMEM((n_heads, HEAD_DIM, HEAD_DIM), F32)]
                       + [pltpu.VMEM((tb, hw), F32) for _ in range(7)],
        compiler_params=pltpu.CompilerParams(dimension_semantics=("arbitrary", "arbitrary"),
                                             vmem_limit_bytes=VMEM_LIMIT_MIX),
        name="hgrn_gmlp_mixer",
    )(x, mod8, g1, w_in, lb, hg, lng, lnb, gng, sw, sbt, w_out)


def _topk_rows(x, k):
    n = x.shape[0]
    rid = lax.broadcasted_iota(I32, x.shape, 0)
    vals, ids = [], []
    for _ in range(k):
        m = jnp.max(x, axis=0, keepdims=True)
        am = jnp.min(jnp.where(x == m, rid, n), axis=0, keepdims=True)
        vals.append(m)
        ids.append(am)
        x = jnp.where(rid == am, -jnp.inf, x)
    return jnp.concatenate(vals, axis=0), jnp.concatenate(ids, axis=0)


def _route_kernel(x_ref, mod_ref, g2_ref, wq_ref, keys_ref, h_ref, idx_ref, gate_ref, *, n_heads):
    x = x_ref[...]
    shift, scale = mod_ref[3:4, :], mod_ref[4:5, :]
    h = (_rms(x) * g2_ref[...]) * (1.0 + scale) + shift
    tb = x.shape[0]
    for c in range(SUBLANES):
        h_ref[pl.ds(c, tb, stride=SUBLANES), :] = h[:, c * LANES:(c + 1) * LANES]
    q = jnp.dot(h.astype(BF16), wq_ref[...], preferred_element_type=F32)
    k = PEER_TOPK
    jrow8 = lax.broadcasted_iota(I32, (SUBLANES, tb), 0)
    idx_rows, gate_rows = [], []
    for hh in range(n_heads):
        tops = []
        for p in range(2):
            col = (hh * 2 + p) * HEAD_DIM
            sim_t = lax.dot_general(keys_ref[hh, p].astype(BF16), q[:, col:col + HEAD_DIM].astype(BF16),
                                    (((1,), (1,)), ((), ())), preferred_element_type=F32)
            tops.append(_topk_rows(sim_t, k))
        (s1, i1), (s2, i2) = tops
        cs = [s1[0:1] + s2]
        ci = [i1[0:1] * N_KEYS + i2]
        for i in range(1, SUBLANES):
            nvalid = k // (i + 1)
            cs.append(jnp.where(jrow8 < nvalid, s1[i:i + 1] + s2[0:SUBLANES], -jnp.inf))
            ci.append(i1[i:i + 1] * N_KEYS + i2[0:SUBLANES])
        cs.append(s1[SUBLANES:k] + s2[0:1])
        ci.append(i1[SUBLANES:k] * N_KEYS + i2[0:1])
        cand_s = jnp.concatenate(cs, axis=0)
        cand_i = jnp.concatenate(ci, axis=0)
        top_s, pos = _topk_rows(cand_s, k)
        prow = lax.broadcasted_iota(I32, cand_s.shape, 0)
        sel = [jnp.sum(jnp.where(prow == pos[r:r + 1], cand_i, 0), axis=0, keepdims=True) for r in range(k)]
        top_i = jnp.concatenate(sel, axis=0)
        ex = jnp.exp(top_s - top_s[0:1])
        gate_rows.append(ex / jnp.sum(ex, axis=0, keepdims=True))
        idx_rows.append(top_i)
    idx_t = jnp.concatenate(idx_rows, axis=0)
    gate_t = jnp.concatenate(gate_rows, axis=0)
    idx_ref[...] = (idx_t * 4).T
    gate_ref[...] = gate_t.T


def _route_call(x1, mod8, g2, wq, keys, *, tb):
    b, s, d = x1.shape
    n_heads = keys.shape[0]
    nsel = n_heads * PEER_TOPK
    nblk = s // tb
    kern = functools.partial(_route_kernel, n_heads=n_heads)
    return pl.pallas_call(
        kern,
        grid=(b, nblk),
        in_specs=[pl.BlockSpec((None, tb, d), lambda i, j: (i, j, 0)),
                  pl.BlockSpec((None, 8, d), lambda i, j: (i, 0, 0)),
                  _const_spec(g2.shape), _const_spec(wq.shape), _const_spec(keys.shape)],
        out_specs=[pl.BlockSpec((tb * SUBLANES, LANES), lambda i, j: (i * nblk + j, 0)),
                   pl.BlockSpec((tb, nsel), lambda i, j: (i * nblk + j, 0)),
                   pl.BlockSpec((tb, nsel), lambda i, j: (i * nblk + j, 0))],
        out_shape=[jax.ShapeDtypeStruct((b * s * SUBLANES, LANES), F32),
                   jax.ShapeDtypeStruct((b * s, nsel), I32),
                   jax.ShapeDtypeStruct((b * s, nsel), F32)],
        compiler_params=pltpu.CompilerParams(dimension_semantics=("arbitrary", "arbitrary"),
                                             vmem_limit_bytes=VMEM_LIMIT_MIX),
        name="peer_route",
    )(x1, mod8, g2, wq, keys)


def _pack_kernel(x_ref, o_ref, stage):
    r = x_ref.shape[0]
    for c in range(SUBLANES):
        stage[pl.ds(c, r, stride=SUBLANES), :] = x_ref[:, c * LANES:(c + 1) * LANES]
    o_ref[...] = pltpu.bitcast(stage[...].astype(BF16), I32)


def _pack_table(tab, *, rows=512):
    n, d = tab.shape
    assert d == SUBLANES * LANES and n % rows == 0
    return pl.pallas_call(
        _pack_kernel,
        grid=(n // rows,),
        in_specs=[pl.BlockSpec((rows, d), lambda i: (i, 0))],
        out_specs=pl.BlockSpec((rows * 4, LANES), lambda i: (i, 0)),
        out_shape=jax.ShapeDtypeStruct((n * 4, LANES), I32),
        scratch_shapes=[pltpu.VMEM((rows * SUBLANES, LANES), F32)],
        name="peer_table_pack",
    )(tab)


def _gather_row(tab_ref, idx_ref, pos):
    off = pl.multiple_of(idx_ref[pos], 4)
    return pltpu.bitcast(tab_ref[pl.ds(off, 4), :], BF16).astype(F32)


def _peer_u_kernel(idx_ref, h_ref, gate_ref, tab_ref, act_ref, e_s, *, tbp, nsel):
    sub = lax.broadcasted_iota(I32, (SUBLANES, LANES), 0)
    m4 = (sub & 4) == 0
    m2 = (sub & 2) == 0
    m1 = (sub & 1) == 0

    def fold(a, b, dist, mask):
        return jnp.where(mask, a + pltpu.roll(a, SUBLANES - dist, 0), b + pltpu.roll(b, dist, 0))

    def products(t, e_ref):
        ids = idx_ref.at[pl.ds(t * nsel, nsel)]
        hv = h_ref[pl.ds(pl.multiple_of(t * SUBLANES, SUBLANES), SUBLANES), :]
        for g in range(nsel // SUBLANES):
            p = [_gather_row(tab_ref, ids, g * SUBLANES + j) * hv for j in range(SUBLANES)]
            a0, a1 = fold(p[0], p[4], 4, m4), fold(p[2], p[6], 4, m4)
            a2, a3 = fold(p[1], p[5], 4, m4), fold(p[3], p[7], 4, m4)
            b0, b1 = fold(a0, a1, 2, m2), fold(a2, a3, 2, m2)
            e_ref[g * SUBLANES:(g + 1) * SUBLANES, :] = fold(b0, b1, 1, m1)

    def finish(t, e_ref):
        s = jnp.sum(e_ref[...].T, axis=0, keepdims=True)
        act_ref[pl.ds(t, 1), :] = _gelu(s) * gate_ref[pl.ds(t, 1), :]

    e_s[...] = jnp.zeros_like(e_s)

    def body(i, carry):
        t = 2 * i
        cur = t & 2
        finish(jnp.maximum(t - 2, 0), e_s.at[cur ^ 2])
        finish(jnp.maximum(t - 1, 0), e_s.at[(cur ^ 2) + 1])
        products(t, e_s.at[cur])
        products(t + 1, e_s.at[cur + 1])
        return carry

    lax.fori_loop(0, tbp // 2, body, 0)
    finish(tbp - 2, e_s.at[(tbp - 2) & 2])
    finish(tbp - 1, e_s.at[((tbp - 2) & 2) + 1])


def _peer_v_kernel(idx_ref, act_ref, tab_ref, y_ref, a0, a1, *, tbp, nsel):
    def spread(t, a_ref):
        a_ref[...] = jnp.broadcast_to(act_ref[pl.ds(t, 1), :], (nsel, nsel)).T

    def mix(t, a_ref):
        ids = idx_ref.at[pl.ds(t * nsel, nsel)]
        accs = [jnp.zeros((SUBLANES, LANES), F32) for _ in range(4)]
        for k in range(nsel):
            w = _gather_row(tab_ref, ids, k)
            a = jnp.broadcast_to(a_ref[k:k + 1, :], (SUBLANES, LANES))
            accs[k % 4] = accs[k % 4] + a * w
        y_ref[pl.ds(pl.multiple_of(t * SUBLANES, SUBLANES), SUBLANES), :] = (accs[0] + accs[1]) + (accs[2] + accs[3])

    spread(0, a0)

    def body(i, carry):
        t = 2 * i
        spread(t + 1, a1)
        mix(t, a0)
        spread(jnp.minimum(t + 2, tbp - 1), a0)
        mix(t + 1, a1)
        return carry

    lax.fori_loop(0, tbp // 2, body, 0)


def _table_spec(shape):
    return pl.BlockSpec(shape, lambda i: (0, 0), pipeline_mode=pl.Buffered(1))


def _peer_u_call(idx_flat, h8, gate, tab, *, tbp):
    t, nsel = gate.shape
    kern = functools.partial(_peer_u_kernel, tbp=tbp, nsel=nsel)
    return pl.pallas_call(
        kern,
        grid=(t // tbp,),
        in_specs=[pl.BlockSpec((tbp * nsel,), lambda i: (i,), memory_space=pltpu.SMEM),
                  pl.BlockSpec((tbp * SUBLANES, LANES), lambda i: (i, 0)),
                  pl.BlockSpec((tbp, nsel), lambda i: (i, 0)),
                  _table_spec(tab.shape)],
        out_specs=pl.BlockSpec((tbp, nsel), lambda i: (i, 0)),
        out_shape=jax.ShapeDtypeStruct((t, nsel), F32),
        scratch_shapes=[pltpu.VMEM((4, nsel, LANES), F32)],
        compiler_params=pltpu.CompilerParams(dimension_semantics=("arbitrary",),
                                             vmem_limit_bytes=VMEM_LIMIT_PEER),
        name="peer_expert_scores",
    )(idx_flat, h8, gate, tab)


def _peer_v_call(idx_flat, act, tab, *, tbp):
    t, nsel = act.shape
    kern = functools.partial(_peer_v_kernel, tbp=tbp, nsel=nsel)
    return pl.pallas_call(
        kern,
        grid=(t // tbp,),
        in_specs=[pl.BlockSpec((tbp * nsel,), lambda i: (i,), memory_space=pltpu.SMEM),
                  pl.BlockSpec((tbp, nsel), lambda i: (i, 0)),
                  _table_spec(tab.shape)],
        out_specs=pl.BlockSpec((tbp * SUBLANES, LANES), lambda i: (i, 0)),
        out_shape=jax.ShapeDtypeStruct((t * SUBLANES, LANES), F32),
        scratch_shapes=[pltpu.VMEM((nsel, nsel), F32), pltpu.VMEM((nsel, nsel), F32)],
        compiler_params=pltpu.CompilerParams(dimension_semantics=("arbitrary",),
                                             vmem_limit_bytes=VMEM_LIMIT_PEER),
        name="peer_expert_mix",
    )(idx_flat, act, tab)


def _resid_kernel(x_ref, y_ref, mod_ref, g_ref, o_ref, *, final):
    tb = x_ref.shape[0]
    y = jnp.concatenate([y_ref[pl.ds(c, tb, stride=SUBLANES), :] for c in range(SUBLANES)], axis=-1)
    z = x_ref[...] + mod_ref[5:6, :] * y
    o_ref[...] = _rms(z) * g_ref[...] if final else z


def _resid_call(x1, y, mod8, g, *, tb, final):
    b, s, d = x1.shape
    nblk = s // tb
    return pl.pallas_call(
        functools.partial(_resid_kernel, final=final),
        grid=(b, nblk),
        in_specs=[pl.BlockSpec((None, tb, d), lambda i, j: (i, j, 0)),
                  pl.BlockSpec((tb * SUBLANES, LANES), lambda i, j: (i * nblk + j, 0)),
                  pl.BlockSpec((None, 8, d), lambda i, j: (i, 0, 0)),
                  _const_spec(g.shape)],
        out_specs=pl.BlockSpec((None, tb, d), lambda i, j: (i, j, 0)),
        out_shape=jax.ShapeDtypeStruct((b, s, d), F32),
        name="peer_residual_norm",
    )(x1, y, mod8, g)


def _pick_block(n, pref):
    while n % pref:
        pref //= 2
    return pref


def kernel(x, c, ada_w, ada_b, norm1_g, w_in, lb_gamma, hgrn_norm_g, gmlp_ln_g, gmlp_ln_b, spatial_w,
           spatial_b, gmlp_norm_g, w_out, norm2_g, peer_wq, peer_keys, peer_u, peer_v, final_g):
    b, s, d = x.shape
    depth = ada_w.shape[0]
    t = b * s
    lower_bounds = jnp.cumsum(jax.nn.softmax(lb_gamma.astype(F32), axis=0), axis=0)
    c_pad = jnp.zeros((8, d), F32).at[:b].set(c)
    tb_mix = _pick_block(s, 256)
    tb_route = _pick_block(s, 256)
    tb_res = _pick_block(s, 512)
    tbp = _pick_block(t, 64)
    for l in range(depth):
        mod = _mod_call(c_pad, ada_w[l], ada_b[l])[:b]
        mod8 = jnp.concatenate([mod.reshape(b, 6, d), jnp.zeros((b, 2, d), F32)], axis=1)
        x1 = _mix_call(x, mod8, norm1_g[l][None], w_in[l].astype(BF16), lower_bounds[l][None],
                       hgrn_norm_g[l][None], gmlp_ln_g[l][None], gmlp_ln_b[l][None], gmlp_norm_g[l][None],
                       spatial_w[l], spatial_b[l].T, w_out[l].astype(BF16), tb=tb_mix)
        h2, idx4, gate = _route_call(x1, mod8, norm2_g[l][None], peer_wq[l].astype(BF16), peer_keys[l],
                                     tb=tb_route)
        idx_flat = idx4.reshape(-1)
        act = _peer_u_call(idx_flat, h2, gate, _pack_table(peer_u[l]), tbp=tbp)
        y8 = _peer_v_call(idx_flat, act, _pack_table(peer_v[l]), tbp=tbp)
        last = l == depth - 1
        x = _resid_call(x1, y8, mod8, final_g[None] if last else jnp.ones((1, d), F32),
                        tb=tb_res, final=last)
    return x
```

```python
import functools

import jax
import jax.numpy as jnp
from jax import lax
from jax.experimental import pallas as pl
from jax.experimental.pallas import tpu as pltpu

F32 = jnp.float32
BF16 = jnp.bfloat16
I32 = jnp.int32
HIGHEST = lax.Precision.HIGHEST
NORM_EPS = 1e-6

LANES = 128
SUBLANES = 8
HEAD_DIM = 128
GMLP_CHUNK = 128
SUB = 16
PEER_TOPK = 16
N_KEYS = 128
IDX_GROUP = 16
IDX_SLOTS = 2
TRIP = IDX_GROUP * IDX_SLOTS
SEL_TILE = 32
ACT_PARTS = 3
VMEM_LIMIT_MIX = 48 * 1024 * 1024
VMEM_LIMIT_PEER = 52 * 1024 * 1024


def _gelu(x):
    return 0.5 * x * (1.0 + jnp.tanh(0.7978845608028654 * (x + 0.044715 * (x * x * x))))


def _sigmoid(x):
    return 1.0 / (1.0 + jnp.exp(-x))


def _rms(x, eps=NORM_EPS):
    return x * lax.rsqrt(jnp.mean(x * x, axis=-1, keepdims=True) + eps)


def _const_spec(shape):
    nd = len(shape)
    return pl.BlockSpec(shape, lambda *_: (0,) * nd)


def _mod_kernel(c_ref, w_ref, b_ref, o_ref):
    c = c_ref[...]
    ca = c * _sigmoid(c)
    o_ref[...] = jnp.dot(ca, w_ref[...], precision=HIGHEST, preferred_element_type=F32) + b_ref[...]


def _mod_call(c_pad, w, b):
    rows, d = c_pad.shape
    n = w.shape[1]
    bn = 1536 if n % 1536 == 0 else n
    return pl.pallas_call(
        _mod_kernel,
        grid=(n // bn,),
        in_specs=[pl.BlockSpec((rows, d), lambda j: (0, 0)),
                  pl.BlockSpec((d, bn), lambda j: (0, j)),
                  pl.BlockSpec((1, bn), lambda j: (0, j))],
        out_specs=pl.BlockSpec((rows, bn), lambda j: (0, j)),
        out_shape=jax.ShapeDtypeStruct((rows, n), F32),
        name="adaln_mod",
    )(c_pad, w, b.reshape(1, n))


def _mix_kernel(x_ref, mod_ref, g1_ref, win_ref, lb_ref, hg_ref, lng_ref, lnb_ref, gng_ref,
                sw_ref, sbt_ref, wout_ref, o_ref,
                state, qe_s, ke_s, qf_s, kg_s, bb_s, vv_s, cat_s, *, tb, hw, gw):
    n_heads = hw // HEAD_DIM
    n_groups = gw // HEAD_DIM

    @pl.when(pl.program_id(1) == 0)
    def _():
        state[...] = jnp.zeros_like(state)

    x = x_ref[...]
    shift, scale, gate = mod_ref[0:1, :], mod_ref[1:2, :], mod_ref[2:3, :]
    h = (_rms(x) * g1_ref[...]) * (1.0 + scale) + shift
    proj = jnp.dot(h.astype(BF16), win_ref[...], preferred_element_type=F32)

    lb = lb_ref[...]
    q = proj[:, 0:hw]
    fz = proj[:, hw:2 * hw]
    e = jnp.exp(-jnp.abs(fz))
    r = 1.0 / (1.0 + e)
    pos = fz >= 0.0
    sig = jnp.where(pos, r, e * r)
    sig_neg = jnp.where(pos, e * r, r)
    logf = jnp.log(lb + (1.0 - lb) * sig)
    kg = (1.0 - lb) * sig_neg
    qf = q * _sigmoid(q)
    ri = lax.broadcasted_iota(I32, (tb, tb), 0)
    ci = lax.broadcasted_iota(I32, (tb, tb), 1)
    same = (ri // SUB) == (ci // SUB)
    low = jnp.where(same & (ci <= ri), 1.0, 0.0).astype(F32)
    upp = jnp.where(same & (ci > ri), 1.0, 0.0).astype(F32)
    bb = jnp.dot(low, logf, precision=HIGHEST, preferred_element_type=F32)
    rem = jnp.dot(upp, logf, precision=HIGHEST, preferred_element_type=F32)
    qe_s[...] = qf * jnp.exp(bb)
    ke_s[...] = kg * jnp.exp(rem)
    qf_s[...] = qf
    kg_s[...] = kg
    bb_s[...] = bb
    vv_s[...] = proj[:, 2 * hw:3 * hw]

    rows16 = lax.broadcasted_iota(I32, (SUB, HEAD_DIM), 0)
    rows128 = lax.broadcasted_iota(I32, (GMLP_CHUNK, HEAD_DIM), 0)
    steps_per_chunk = GMLP_CHUNK // SUB

    def chunk_body(c, carry):
        c0 = pl.multiple_of(c * GMLP_CHUNK, GMLP_CHUNK)
        for hd in range(n_heads):
            cs = slice(hd * HEAD_DIM, (hd + 1) * HEAD_DIM)
            qe_c = qe_s[pl.ds(c0, GMLP_CHUNK), cs]
            ke_c = ke_s[pl.ds(c0, GMLP_CHUNK), cs]
            qf_c = qf_s[pl.ds(c0, GMLP_CHUNK), cs]
            kg_c = kg_s[pl.ds(c0, GMLP_CHUNK), cs]
            bb_c = bb_s[pl.ds(c0, GMLP_CHUNK), cs]
            v_c = vv_s[pl.ds(c0, GMLP_CHUNK), cs]
            v_t = v_c.T.astype(BF16)
            st = state[hd]
            outs = []
            for j in range(steps_per_chunk):
                rs = slice(j * SUB, (j + 1) * SUB)
                bb_b, qf_b, kg_b, v_b = bb_c[rs], qf_c[rs], kg_c[rs], v_c[rs]
                o = lax.dot_general(qe_c[rs].astype(BF16), st.astype(BF16), (((1,), (1,)), ((), ())),
                                    preferred_element_type=F32)
                for s in range(SUB):
                    dec = jnp.where(rows16 >= s, jnp.exp(bb_b - bb_b[s:s + 1, :]), 0.0)
                    w = jnp.sum(qf_b * dec * kg_b[s:s + 1, :], axis=-1, keepdims=True)
                    o = o + w * v_b[s:s + 1, :]
                outs.append(o)
                in_step = (rows128 >= j * SUB) & (rows128 < (j + 1) * SUB)
                ke_m = jnp.where(in_step, ke_c, 0.0).astype(BF16)
                upd = jnp.dot(v_t, ke_m, preferred_element_type=F32)
                st = jnp.exp(bb_b[SUB - 1:SUB, :]) * st + upd
            state[hd] = st
            o_c = jnp.concatenate(outs, axis=0)
            cat_s[pl.ds(c0, GMLP_CHUNK), cs] = _rms(o_c) * hg_ref[:, cs]
        return carry

    lax.fori_loop(0, tb // GMLP_CHUNK, chunk_body, 0)

    og = proj[:, 3 * hw:4 * hw]
    o_all = cat_s[:, 0:hw] * (og * _sigmoid(og))

    u = _gelu(proj[:, 4 * hw:4 * hw + gw])
    v = _gelu(proj[:, 4 * hw + gw:4 * hw + 2 * gw])
    mu = jnp.mean(v, axis=-1, keepdims=True)
    vc = v - mu
    var = jnp.mean(vc * vc, axis=-1, keepdims=True)
    vln = (vc * lax.rsqrt(var + NORM_EPS)) * lng_ref[...] + lnb_ref[...]
    tri = (lax.broadcasted_iota(I32, (GMLP_CHUNK, GMLP_CHUNK), 1)
           <= lax.broadcasted_iota(I32, (GMLP_CHUNK, GMLP_CHUNK), 0))
    gm_groups = []
    for g in range(n_groups):
        gs = slice(g * HEAD_DIM, (g + 1) * HEAD_DIM)
        wm = jnp.where(tri, sw_ref[g], 0.0).astype(BF16)
        bias = sbt_ref[:, g:g + 1]
        parts = []
        for cidx in range(tb // GMLP_CHUNK):
            rs = slice(cidx * GMLP_CHUNK, (cidx + 1) * GMLP_CHUNK)
            mixed = jnp.dot(wm, vln[rs, gs].astype(BF16), preferred_element_type=F32) + bias
            parts.append(u[rs, gs] * mixed)
        gmg = jnp.concatenate(parts, axis=0) if len(parts) > 1 else parts[0]
        gm_groups.append(_rms(gmg) * gng_ref[:, gs])
    gm = jnp.concatenate(gm_groups, axis=-1)

    cat = jnp.concatenate([o_all, gm], axis=-1).astype(BF16)
    mixed_out = jnp.dot(cat, wout_ref[...], preferred_element_type=F32)
    o_ref[...] = x + gate * mixed_out


def _mix_call(x, mod8, g1, w_in, lb, hg, lng, lnb, gng, sw, sbt, w_out, *, tb):
    b, s, d = x.shape
    hw = lb.shape[1]
    gw = lng.shape[1]
    n_heads = hw // HEAD_DIM
    kern = functools.partial(_mix_kernel, tb=tb, hw=hw, gw=gw)
    return pl.pallas_call(
        kern,
        grid=(b, s // tb),
        in_specs=[pl.BlockSpec((None, tb, d), lambda i, j: (i, j, 0)),
                  pl.BlockSpec((None, 8, d), lambda i, j: (i, 0, 0)),
                  _const_spec(g1.shape), _const_spec(w_in.shape), _const_spec(lb.shape),
                  _const_spec(hg.shape), _const_spec(lng.shape), _const_spec(lnb.shape),
                  _const_spec(gng.shape), _const_spec(sw.shape), _const_spec(sbt.shape),
                  _const_spec(w_out.shape)],
        out_specs=pl.BlockSpec((None, tb, d), lambda i, j: (i, j, 0)),
        out_shape=jax.ShapeDtypeStruct((b, s, d), F32),
        scratch_shapes=[pltpu.VMEM((n_heads, HEAD_DIM, HEAD_DIM), F32)]
                       + [pltpu.VMEM((tb, hw), F32) for _ in range(7)],
        compiler_params=pltpu.CompilerParams(dimension_semantics=("arbitrary", "arbitrary"),
                                             vmem_limit_bytes=VMEM_LIMIT_MIX),
        name="hgrn_gmlp_mixer",
    )(x, mod8, g1, w_in, lb, hg, lng, lnb, gng, sw, sbt, w_out)


def _topk_rows(x, k):
    n = x.shape[0]
    rid = lax.broadcasted_iota(I32, x.shape, 0)
    vals, ids = [], []
    for _ in range(k):
        m = jnp.max(x, axis=0, keepdims=True)
        am = jnp.min(jnp.where(x == m, rid, n), axis=0, keepdims=True)
        vals.append(m)
        ids.append(am)
        x = jnp.where(rid == am, -jnp.inf, x)
    return jnp.concatenate(vals, axis=0), jnp.concatenate(ids, axis=0)


def _route_kernel(x_ref, mod_ref, g2_ref, wq_ref, keys_ref, h_ref, idx_ref, gate_ref, *, n_heads):
    x = x_ref[...]
    shift, scale = mod_ref[3:4, :], mod_ref[4:5, :]
    h = (_rms(x) * g2_ref[...]) * (1.0 + scale) + shift
    tb = x.shape[0]
    for c in range(SUBLANES):
        h_ref[pl.ds(c, tb, stride=SUBLANES), :] = h[:, c * LANES:(c + 1) * LANES]
    q = jnp.dot(h.astype(BF16), wq_ref[...], preferred_element_type=F32)
    k = PEER_TOPK
    jrow8 = lax.broadcasted_iota(I32, (SUBLANES, tb), 0)
    idx_rows, gate_rows = [], []
    for hh in range(n_heads):
        tops = []
        for p in range(2):
            col = (hh * 2 + p) * HEAD_DIM
            sim_t = lax.dot_general(keys_ref[hh, p].astype(BF16), q[:, col:col + HEAD_DIM].astype(BF16),
                                    (((1,), (1,)), ((), ())), preferred_element_type=F32)
            tops.append(_topk_rows(sim_t, k))
        (s1, i1), (s2, i2) = tops
        cs = [s1[0:1] + s2]
        ci = [i1[0:1] * N_KEYS + i2]
        for i in range(1, SUBLANES):
            nvalid = k // (i + 1)
            cs.append(jnp.where(jrow8 < nvalid, s1[i:i + 1] + s2[0:SUBLANES], -jnp.inf))
            ci.append(i1[i:i + 1] * N_KEYS + i2[0:SUBLANES])
        cs.append(s1[SUBLANES:k] + s2[0:1])
        ci.append(i1[SUBLANES:k] * N_KEYS + i2[0:1])
        cand_s = jnp.concatenate(cs, axis=0)
        cand_i = jnp.concatenate(ci, axis=0)
        top_s, pos = _topk_rows(cand_s, k)
        prow = lax.broadcasted_iota(I32, cand_s.shape, 0)
        sel = [jnp.sum(jnp.where(prow == pos[r:r + 1], cand_i, 0), axis=0, keepdims=True) for r in range(k)]
        top_i = jnp.concatenate(sel, axis=0)
        ex = jnp.exp(top_s - top_s[0:1])
        gate_rows.append(ex / jnp.sum(ex, axis=0, keepdims=True))
        idx_rows.append(top_i)
    idx_t = jnp.concatenate(idx_rows, axis=0)
    gate_t = jnp.concatenate([gate_rows[hh][r:r + 1] for r in range(k) for hh in range(n_heads)], axis=0)
    idx_ref[...] = (idx_t * 4).T
    gate_ref[...] = gate_t.T


def _route_call(x1, mod8, g2, wq, keys, *, tb):
    b, s, d = x1.shape
    n_heads = keys.shape[0]
    nsel = n_heads * PEER_TOPK
    nblk = s // tb
    kern = functools.partial(_route_kernel, n_heads=n_heads)
    return pl.pallas_call(
        kern,
        grid=(b, nblk),
        in_specs=[pl.BlockSpec((None, tb, d), lambda i, j: (i, j, 0)),
                  pl.BlockSpec((None, 8, d), lambda i, j: (i, 0, 0)),
                  _const_spec(g2.shape), _const_spec(wq.shape), _const_spec(keys.shape)],
        out_specs=[pl.BlockSpec((tb * SUBLANES, LANES), lambda i, j: (i * nblk + j, 0)),
                   pl.BlockSpec((tb, nsel), lambda i, j: (i * nblk + j, 0)),
                   pl.BlockSpec((tb, nsel), lambda i, j: (i * nblk + j, 0))],
        out_shape=[jax.ShapeDtypeStruct((b * s * SUBLANES, LANES), F32),
                   jax.ShapeDtypeStruct((b * s, nsel), I32),
                   jax.ShapeDtypeStruct((b * s, nsel), F32)],
        compiler_params=pltpu.CompilerParams(dimension_semantics=("arbitrary", "arbitrary"),
                                             vmem_limit_bytes=VMEM_LIMIT_MIX),
        name="peer_route",
    )(x1, mod8, g2, wq, keys)


def _pack_kernel(x_ref, o_ref, stage):
    r = x_ref.shape[0]
    for c in range(SUBLANES):
        stage[pl.ds(c, r, stride=SUBLANES), :] = x_ref[:, c * LANES:(c + 1) * LANES]
    o_ref[...] = pltpu.bitcast(stage[...].astype(BF16), I32)


def _pack_table(tab, *, rows=512):
    n, d = tab.shape
    assert d == SUBLANES * LANES and n % rows == 0
    return pl.pallas_call(
        _pack_kernel,
        grid=(n // rows,),
        in_specs=[pl.BlockSpec((rows, d), lambda i: (i, 0))],
        out_specs=pl.BlockSpec((rows * 4, LANES), lambda i: (i, 0)),
        out_shape=jax.ShapeDtypeStruct((n * 4, LANES), I32),
        scratch_shapes=[pltpu.VMEM((rows * SUBLANES, LANES), F32)],
        name="peer_table_pack",
    )(tab)


def _gather_tile(tab_ref, ibuf, slot, positions):
    pieces = [tab_ref[pl.ds(pl.multiple_of(ibuf[slot, p], 4), 4), :] for p in positions]
    return pltpu.bitcast(jnp.concatenate(pieces, axis=0), BF16)


def _stage_ids(idx_hbm, ibuf, sem, first_token, slot, nsel):
    n = IDX_GROUP * nsel
    return pltpu.make_async_copy(idx_hbm.at[pl.ds(first_token * nsel, n)], ibuf.at[slot], sem.at[slot])


def _staged_token_loop(idx_hbm, ibuf, sem, tbp, nsel, open_trip):
    tok_base = pl.program_id(0) * tbp
    ngroups = tbp // IDX_GROUP

    def stage(g, slot):
        return _stage_ids(idx_hbm, ibuf, sem, tok_base + g * IDX_GROUP, slot, nsel)

    for s in range(IDX_SLOTS - 1):
        stage(s, s).start()

    def trip(i, carry):
        per_token = open_trip(i)
        for s in range(IDX_SLOTS):
            g = i * IDX_SLOTS + s
            stage(g, s).wait()
            stage(jnp.minimum(g + IDX_SLOTS - 1, ngroups - 1), (s + IDX_SLOTS - 1) % IDX_SLOTS).start()
            for u in range(IDX_GROUP):
                per_token(s * IDX_GROUP + u, s, u)
        return carry

    lax.fori_loop(0, ngroups // IDX_SLOTS, trip, 0)
    for s in range(IDX_SLOTS - 1):
        stage(ngroups - 1, s).wait()


def _window(ref, first, n):
    return ref.at[pl.ds(pl.multiple_of(first, SUBLANES), n)]


def _split_bf16(x, parts):
    out = []
    for _ in range(parts - 1):
        p = x.astype(BF16)
        out.append(p)
        x = x - p.astype(F32)
    out.append(x.astype(BF16))
    return out


def _sel_order(q):
    return (q % SUBLANES) * PEER_TOPK + q // SUBLANES


def _peer_u_kernel(idx_hbm, h_ref, gate_ref, tab_ref, act_ref, ibuf, sem, *, tbp, nsel):
    sub = lax.broadcasted_iota(I32, (SUBLANES, LANES), 0)
    lane = lax.broadcasted_iota(I32, (SUBLANES, LANES), 1)
    diag = sub == (lane & (SUBLANES - 1))
    lane0 = (lane & (SUBLANES - 1)) == 0
    m4 = (sub & 4) == 0
    m2 = (sub & 2) == 0
    m1 = (sub & 1) == 0

    def fold(a, b, dist, mask):
        return jnp.where(mask, a + pltpu.roll(a, SUBLANES - dist, 0), b + pltpu.roll(b, dist, 0))

    def scores(h_win, tt, slot, u):
        hv = h_win[tt * SUBLANES:(tt + 1) * SUBLANES, :]
        h16 = jnp.concatenate(_split_bf16(hv, 2), axis=0)
        z = []
        for j in range(nsel // SEL_TILE):
            m = _gather_tile(tab_ref, ibuf, slot, [u * nsel + j * SEL_TILE + i for i in range(SEL_TILE)])
            o = lax.dot_general(h16, m, (((1,), (1,)), ((), ())), preferred_element_type=F32)
            o = o[0:SUBLANES] + o[SUBLANES:2 * SUBLANES]
            z += [jnp.where(diag, o[:, c * LANES:(c + 1) * LANES], 0.0) for c in range(SEL_TILE * SUBLANES // LANES)]
        a0, a1 = fold(z[0], z[4], 4, m4), fold(z[2], z[6], 4, m4)
        a2, a3 = fold(z[1], z[5], 4, m4), fold(z[3], z[7], 4, m4)
        f = fold(fold(a0, a1, 2, m2), fold(a2, a3, 2, m2), 1, m1)
        for sh in (4, 2, 1):
            f = f + pltpu.roll(f, LANES - sh, 1)
        f = jnp.where(lane0, f, 0.0)
        f = pltpu.roll(f, 0, 1, stride=1, stride_axis=0)
        return jnp.sum(f, axis=0, keepdims=True)

    def open_trip(i):
        tok0 = i * TRIP
        h_win = _window(h_ref, tok0 * SUBLANES, TRIP * SUBLANES)
        act_win, gate_win = _window(act_ref, tok0, TRIP), _window(gate_ref, tok0, TRIP)

        def per_token(tt, slot, u):
            act_win[tt:tt + 1, :] = _gelu(scores(h_win, tt, slot, u)) * gate_win[tt:tt + 1, :]

        return per_token

    _staged_token_loop(idx_hbm, ibuf, sem, tbp, nsel, open_trip)


def _peer_v_kernel(idx_hbm, act_ref, tab_ref, rep_ref, y_ref, ibuf, sem, arep_s, *, tbp, nsel):
    width = nsel * SUBLANES
    diag = (lax.broadcasted_iota(I32, (SUBLANES, width), 0)
            == (lax.broadcasted_iota(I32, (SUBLANES, width), 1) & (SUBLANES - 1)))

    def open_trip(i):
        tok0 = i * TRIP
        parts = jnp.concatenate(_split_bf16(act_ref[pl.ds(pl.multiple_of(tok0, TRIP), TRIP), :], ACT_PARTS), axis=0)
        arep_s[...] = jnp.dot(parts, rep_ref[...], preferred_element_type=F32)
        y_win = _window(y_ref, tok0 * SUBLANES, TRIP * SUBLANES)

        def per_token(tt, slot, u):
            lhs = jnp.concatenate(
                [jnp.where(diag, jnp.broadcast_to(arep_s[p * TRIP + tt:p * TRIP + tt + 1, :], (SUBLANES, width)), 0.0)
                 for p in range(ACT_PARTS)], axis=0).astype(BF16)
            acc = None
            for j in range(nsel // SEL_TILE):
                m = _gather_tile(tab_ref, ibuf, slot,
                                 [u * nsel + _sel_order(j * SEL_TILE + i) for i in range(SEL_TILE)])
                d = jnp.dot(lhs[:, j * SEL_TILE * SUBLANES:(j + 1) * SEL_TILE * SUBLANES], m,
                            preferred_element_type=F32)
                acc = d if acc is None else acc + d
            y = acc[0:SUBLANES]
            for p in range(1, ACT_PARTS):
                y = y + acc[p * SUBLANES:(p + 1) * SUBLANES]
            y_win[tt * SUBLANES:(tt + 1) * SUBLANES, :] = y

        return per_token

    _staged_token_loop(idx_hbm, ibuf, sem, tbp, nsel, open_trip)


def _table_spec(shape):
    return pl.BlockSpec(shape, lambda i: (0, 0), pipeline_mode=pl.Buffered(1))


def _stage_scratch(nsel):
    return [pltpu.SMEM((IDX_SLOTS, IDX_GROUP * nsel), I32), pltpu.SemaphoreType.DMA((IDX_SLOTS,))]


def _peer_u_call(idx_flat, h8, gate, tab, *, tbp):
    t, nsel = gate.shape
    kern = functools.partial(_peer_u_kernel, tbp=tbp, nsel=nsel)
    return pl.pallas_call(
        kern,
        grid=(t // tbp,),
        in_specs=[pl.BlockSpec(memory_space=pl.ANY),
                  pl.BlockSpec((tbp * SUBLANES, LANES), lambda i: (i, 0)),
                  pl.BlockSpec((tbp, nsel), lambda i: (i, 0)),
                  _table_spec(tab.shape)],
        out_specs=pl.BlockSpec((tbp, nsel), lambda i: (i, 0)),
        out_shape=jax.ShapeDtypeStruct((t, nsel), F32),
        scratch_shapes=_stage_scratch(nsel),
        compiler_params=pltpu.CompilerParams(dimension_semantics=("arbitrary",),
                                             vmem_limit_bytes=VMEM_LIMIT_PEER),
        name="peer_expert_scores",
    )(idx_flat, h8, gate, tab)


def _peer_v_call(idx_flat, act, tab, *, tbp):
    t, nsel = act.shape
    kern = functools.partial(_peer_v_kernel, tbp=tbp, nsel=nsel)
    rep = jnp.repeat(jnp.eye(nsel, dtype=BF16), SUBLANES, axis=1)
    return pl.pallas_call(
        kern,
        grid=(t // tbp,),
        in_specs=[pl.BlockSpec(memory_space=pl.ANY),
                  pl.BlockSpec((tbp, nsel), lambda i: (i, 0)),
                  _table_spec(tab.shape),
                  pl.BlockSpec(rep.shape, lambda i: (0, 0))],
        out_specs=pl.BlockSpec((tbp * SUBLANES, LANES), lambda i: (i, 0)),
        out_shape=jax.ShapeDtypeStruct((t * SUBLANES, LANES), F32),
        scratch_shapes=_stage_scratch(nsel) + [pltpu.VMEM((ACT_PARTS * TRIP, nsel * SUBLANES), F32)],
        compiler_params=pltpu.CompilerParams(dimension_semantics=("arbitrary",),
                                             vmem_limit_bytes=VMEM_LIMIT_PEER),
        name="peer_expert_mix",
    )(idx_flat, act, tab, rep)


def _resid_kernel(x_ref, y_ref, mod_ref, g_ref, o_ref, *, final):
    tb = x_ref.shape[0]
    y = jnp.concatenate([y_ref[pl.ds(c, tb, stride=SUBLANES), :] for c in range(SUBLANES)], axis=-1)
    z = x_ref[...] + mod_ref[5:6, :] * y
    o_ref[...] = _rms(z) * g_ref[...] if final else z


def _resid_call(x1, y, mod8, g, *, tb, final):
    b, s, d = x1.shape
    nblk = s // tb
    return pl.pallas_call(
        functools.partial(_resid_kernel, final=final),
        grid=(b, nblk),
        in_specs=[pl.BlockSpec((None, tb, d), lambda i, j: (i, j, 0)),
                  pl.BlockSpec((tb * SUBLANES, LANES), lambda i, j: (i * nblk + j, 0)),
                  pl.BlockSpec((None, 8, d), lambda i, j: (i, 0, 0)),
                  _const_spec(g.shape)],
        out_specs=pl.BlockSpec((None, tb, d), lambda i, j: (i, j, 0)),
        out_shape=jax.ShapeDtypeStruct((b, s, d), F32),
        name="peer_residual_norm",
    )(x1, y, mod8, g)


def _pick_block(n, pref):
    while n % pref:
        pref //= 2
    return pref


def kernel(x, c, ada_w, ada_b, norm1_g, w_in, lb_gamma, hgrn_norm_g, gmlp_ln_g, gmlp_ln_b, spatial_w,
           spatial_b, gmlp_norm_g, w_out, norm2_g, peer_wq, peer_keys, peer_u, peer_v, final_g):
    b, s, d = x.shape
    depth = ada_w.shape[0]
    t = b * s
    lower_bounds = jnp.cumsum(jax.nn.softmax(lb_gamma.astype(F32), axis=0), axis=0)
    c_pad = jnp.zeros((8, d), F32).at[:b].set(c)
    tb_mix = _pick_block(s, 256)
    tb_route = _pick_block(s, 256)
    tb_res = _pick_block(s, 512)
    tbp = _pick_block(t, 512)
    assert tbp % TRIP == 0
    for l in range(depth):
        mod = _mod_call(c_pad, ada_w[l], ada_b[l])[:b]
        mod8 = jnp.concatenate([mod.reshape(b, 6, d), jnp.zeros((b, 2, d), F32)], axis=1)
        x1 = _mix_call(x, mod8, norm1_g[l][None], w_in[l].astype(BF16), lower_bounds[l][None],
                       hgrn_norm_g[l][None], gmlp_ln_g[l][None], gmlp_ln_b[l][None], gmlp_norm_g[l][None],
                       spatial_w[l], spatial_b[l].T, w_out[l].astype(BF16), tb=tb_mix)
        h2, idx4, gate = _route_call(x1, mod8, norm2_g[l][None], peer_wq[l].astype(BF16), peer_keys[l],
                                     tb=tb_route)
        idx_flat = idx4.reshape(-1)
        act = _peer_u_call(idx_flat, h2, gate, _pack_table(peer_u[l]), tbp=tbp)
        y8 = _peer_v_call(idx_flat, act, _pack_table(peer_v[l]), tbp=tbp)
        last = l == depth - 1
        x = _resid_call(x1, y8, mod8, final_g[None] if last else jnp.ones((1, d), F32),
                        tb=tb_res, final=last)
    return x
```

```python
import functools

import jax
import jax.numpy as jnp
from jax import lax
from jax.experimental import pallas as pl
from jax.experimental.pallas import tpu as pltpu

F32 = jnp.float32
BF16 = jnp.bfloat16
I32 = jnp.int32
HIGHEST = lax.Precision.HIGHEST
NORM_EPS = 1e-6

LANES = 128
SUBLANES = 8
HEAD_DIM = 128
GMLP_CHUNK = 128
SUB = 16
PEER_TOPK = 16
N_KEYS = 128
IDX_GROUP = 16
IDX_SLOTS = 2
TRIP = IDX_GROUP * IDX_SLOTS
SEL_TILE = 32
ACT_PARTS = 3
VMEM_LIMIT_MIX = 48 * 1024 * 1024
VMEM_LIMIT_PEER = 52 * 1024 * 1024


def _gelu(x):
    return 0.5 * x * (1.0 + jnp.tanh(0.7978845608028654 * (x + 0.044715 * (x * x * x))))


def _sigmoid(x):
    return 1.0 / (1.0 + jnp.exp(-x))


def _rms(x, eps=NORM_EPS):
    return x * lax.rsqrt(jnp.mean(x * x, axis=-1, keepdims=True) + eps)


def _const_spec(shape):
    nd = len(shape)
    return pl.BlockSpec(shape, lambda *_: (0,) * nd)


def _mod_kernel(c_ref, w_ref, b_ref, o_ref):
    c = c_ref[...]
    ca = c * _sigmoid(c)
    o_ref[...] = jnp.dot(ca, w_ref[...], precision=HIGHEST, preferred_element_type=F32) + b_ref[...]


def _mod_call(c_pad, w, b):
    rows, d = c_pad.shape
    n = w.shape[1]
    bn = 1536 if n % 1536 == 0 else n
    return pl.pallas_call(
        _mod_kernel,
        grid=(n // bn,),
        in_specs=[pl.BlockSpec((rows, d), lambda j: (0, 0)),
                  pl.BlockSpec((d, bn), lambda j: (0, j)),
                  pl.BlockSpec((1, bn), lambda j: (0, j))],
        out_specs=pl.BlockSpec((rows, bn), lambda j: (0, j)),
        out_shape=jax.ShapeDtypeStruct((rows, n), F32),
        name="adaln_mod",
    )(c_pad, w, b.reshape(1, n))


def _mix_kernel(x_ref, mod_ref, g1_ref, win_ref, lb_ref, hg_ref, lng_ref, lnb_ref, gng_ref,
                sw_ref, sbt_ref, wout_ref, o_ref,
                state, qe_s, ke_s, qf_s, kg_s, bb_s, vv_s, cat_s, *, tb, hw, gw):
    n_heads = hw // HEAD_DIM
    n_groups = gw // HEAD_DIM

    @pl.when(pl.program_id(1) == 0)
    def _():
        state[...] = jnp.zeros_like(state)

    x = x_ref[...]
    shift, scale, gate = mod_ref[0:1, :], mod_ref[1:2, :], mod_ref[2:3, :]
    h = (_rms(x) * g1_ref[...]) * (1.0 + scale) + shift
    proj = jnp.dot(h.astype(BF16), win_ref[...], preferred_element_type=F32)

    lb = lb_ref[...]
    q = proj[:, 0:hw]
    fz = proj[:, hw:2 * hw]
    e = jnp.exp(-jnp.abs(fz))
    r = 1.0 / (1.0 + e)
    pos = fz >= 0.0
    sig = jnp.where(pos, r, e * r)
    sig_neg = jnp.where(pos, e * r, r)
    logf = jnp.log(lb + (1.0 - lb) * sig)
    kg = (1.0 - lb) * sig_neg
    qf = q * _sigmoid(q)
    ri = lax.broadcasted_iota(I32, (tb, tb), 0)
    ci = lax.broadcasted_iota(I32, (tb, tb), 1)
    same = (ri // SUB) == (ci // SUB)
    low = jnp.where(same & (ci <= ri), 1.0, 0.0).astype(F32)
    upp = jnp.where(same & (ci > ri), 1.0, 0.0).astype(F32)
    bb = jnp.dot(low, logf, precision=HIGHEST, preferred_element_type=F32)
    rem = jnp.dot(upp, logf, precision=HIGHEST, preferred_element_type=F32)
    qe_s[...] = qf * jnp.exp(bb)
    ke_s[...] = kg * jnp.exp(rem)
    qf_s[...] = qf
    kg_s[...] = kg
    bb_s[...] = bb
    vv_s[...] = proj[:, 2 * hw:3 * hw]

    rows16 = lax.broadcasted_iota(I32, (SUB, HEAD_DIM), 0)
    rows128 = lax.broadcasted_iota(I32, (GMLP_CHUNK, HEAD_DIM), 0)
    steps_per_chunk = GMLP_CHUNK // SUB

    def chunk_body(c, carry):
        c0 = pl.multiple_of(c * GMLP_CHUNK, GMLP_CHUNK)
        for hd in range(n_heads):
            cs = slice(hd * HEAD_DIM, (hd + 1) * HEAD_DIM)
            qe_c = qe_s[pl.ds(c0, GMLP_CHUNK), cs]
            ke_c = ke_s[pl.ds(c0, GMLP_CHUNK), cs]
            qf_c = qf_s[pl.ds(c0, GMLP_CHUNK), cs]
            kg_c = kg_s[pl.ds(c0, GMLP_CHUNK), cs]
            bb_c = bb_s[pl.ds(c0, GMLP_CHUNK), cs]
            v_c = vv_s[pl.ds(c0, GMLP_CHUNK), cs]
            v_t = v_c.T.astype(BF16)
            st = state[hd]
            outs = []
            for j in range(steps_per_chunk):
                rs = slice(j * SUB, (j + 1) * SUB)
                bb_b, qf_b, kg_b, v_b = bb_c[rs], qf_c[rs], kg_c[rs], v_c[rs]
                o = lax.dot_general(qe_c[rs].astype(BF16), st.astype(BF16), (((1,), (1,)), ((), ())),
                                    preferred_element_type=F32)
                for s in range(SUB):
                    dec = jnp.where(rows16 >= s, jnp.exp(bb_b - bb_b[s:s + 1, :]), 0.0)
                    w = jnp.sum(qf_b * dec * kg_b[s:s + 1, :], axis=-1, keepdims=True)
                    o = o + w * v_b[s:s + 1, :]
                outs.append(o)
                in_step = (rows128 >= j * SUB) & (rows128 < (j + 1) * SUB)
                ke_m = jnp.where(in_step, ke_c, 0.0).astype(BF16)
                upd = jnp.dot(v_t, ke_m, preferred_element_type=F32)
                st = jnp.exp(bb_b[SUB - 1:SUB, :]) * st + upd
            state[hd] = st
            o_c = jnp.concatenate(outs, axis=0)
            cat_s[pl.ds(c0, GMLP_CHUNK), cs] = _rms(o_c) * hg_ref[:, cs]
        return carry

    lax.fori_loop(0, tb // GMLP_CHUNK, chunk_body, 0)

    og = proj[:, 3 * hw:4 * hw]
    o_all = cat_s[:, 0:hw] * (og * _sigmoid(og))

    u = _gelu(proj[:, 4 * hw:4 * hw + gw])
    v = _gelu(proj[:, 4 * hw + gw:4 * hw + 2 * gw])
    mu = jnp.mean(v, axis=-1, keepdims=True)
    vc = v - mu
    var = jnp.mean(vc * vc, axis=-1, keepdims=True)
    vln = (vc * lax.rsqrt(var + NORM_EPS)) * lng_ref[...] + lnb_ref[...]
    tri = (lax.broadcasted_iota(I32, (GMLP_CHUNK, GMLP_CHUNK), 1)
           <= lax.broadcasted_iota(I32, (GMLP_CHUNK, GMLP_CHUNK), 0))
    gm_groups = []
    for g in range(n_groups):
        gs = slice(g * HEAD_DIM, (g + 1) * HEAD_DIM)
        wm = jnp.where(tri, sw_ref[g], 0.0).astype(BF16)
        bias = sbt_ref[:, g:g + 1]
        parts = []
        for cidx in range(tb // GMLP_CHUNK):
            rs = slice(cidx * GMLP_CHUNK, (cidx + 1) * GMLP_CHUNK)
            mixed = jnp.dot(wm, vln[rs, gs].astype(BF16), preferred_element_type=F32) + bias
            parts.append(u[rs, gs] * mixed)
        gmg = jnp.concatenate(parts, axis=0) if len(parts) > 1 else parts[0]
        gm_groups.append(_rms(gmg) * gng_ref[:, gs])
    gm = jnp.concatenate(gm_groups, axis=-1)

    cat = jnp.concatenate([o_all, gm], axis=-1).astype(BF16)
    mixed_out = jnp.dot(cat, wout_ref[...], preferred_element_type=F32)
    o_ref[...] = x + gate * mixed_out


def _mix_call(x, mod8, g1, w_in, lb, hg, lng, lnb, gng, sw, sbt, w_out, *, tb):
    b, s, d = x.shape
    hw = lb.shape[1]
    gw = lng.shape[1]
    n_heads = hw // HEAD_DIM
    kern = functools.partial(_mix_kernel, tb=tb, hw=hw, gw=gw)
    return pl.pallas_call(
        kern,
        grid=(b, s // tb),
        in_specs=[pl.BlockSpec((None, tb, d), lambda i, j: (i, j, 0)),
                  pl.BlockSpec((None, 8, d), lambda i, j: (i, 0, 0)),
                  _const_spec(g1.shape), _const_spec(w_in.shape), _const_spec(lb.shape),
                  _const_spec(hg.shape), _const_spec(lng.shape), _const_spec(lnb.shape),
                  _const_spec(gng.shape), _const_spec(sw.shape), _const_spec(sbt.shape),
                  _const_spec(w_out.shape)],
        out_specs=pl.BlockSpec((None, tb, d), lambda i, j: (i, j, 0)),
        out_shape=jax.ShapeDtypeStruct((b, s, d), F32),
        scratch_shapes=[pltpu.VMEM((n_heads, HEAD_DIM, HEAD_DIM), F32)]
                       + [pltpu.VMEM((tb, hw), F32) for _ in range(7)],
        compiler_params=pltpu.CompilerParams(dimension_semantics=("arbitrary", "arbitrary"),
                                             vmem_limit_bytes=VMEM_LIMIT_MIX),
        name="hgrn_gmlp_mixer",
    )(x, mod8, g1, w_in, lb, hg, lng, lnb, gng, sw, sbt, w_out)


def _topk_rows(x, k):
    n = x.shape[0]
    rid = lax.broadcasted_iota(I32, x.shape, 0).astype(F32)
    vals, ids = [], []
    for _ in range(k):
        m = jnp.max(x, axis=0, keepdims=True)
        am = jnp.min(jnp.where(x == m, rid, float(n)), axis=0, keepdims=True)
        vals.append(m)
        ids.append(am)
        x = jnp.where(rid == am, -jnp.inf, x)
    return jnp.concatenate(vals, axis=0), jnp.concatenate(ids, axis=0)


def _route_kernel(x_ref, mod_ref, g2_ref, wq_ref, keys_ref, h_ref, idx_ref, gate_ref, *, n_heads):
    x = x_ref[...]
    shift, scale = mod_ref[3:4, :], mod_ref[4:5, :]
    h = (_rms(x) * g2_ref[...]) * (1.0 + scale) + shift
    tb = x.shape[0]
    for c in range(SUBLANES):
        h_ref[pl.ds(c, tb, stride=SUBLANES), :] = h[:, c * LANES:(c + 1) * LANES]
    q = jnp.dot(h.astype(BF16), wq_ref[...], preferred_element_type=F32)
    k = PEER_TOPK
    jrow8 = lax.broadcasted_iota(I32, (SUBLANES, tb), 0)
    idx_rows, gate_rows = [], []
    for hh in range(n_heads):
        tops = []
        for p in range(2):
            col = (hh * 2 + p) * HEAD_DIM
            sim_t = lax.dot_general(keys_ref[hh, p].astype(BF16), q[:, col:col + HEAD_DIM].astype(BF16),
                                    (((1,), (1,)), ((), ())), preferred_element_type=F32)
            tops.append(_topk_rows(sim_t, k))
        (s1, i1), (s2, i2) = tops
        cs = [s1[0:1] + s2]
        ci = [i1[0:1] * N_KEYS + i2]
        for i in range(1, SUBLANES):
            nvalid = k // (i + 1)
            cs.append(jnp.where(jrow8 < nvalid, s1[i:i + 1] + s2[0:SUBLANES], -jnp.inf))
            ci.append(i1[i:i + 1] * N_KEYS + i2[0:SUBLANES])
        cs.append(s1[SUBLANES:k] + s2[0:1])
        ci.append(i1[SUBLANES:k] * N_KEYS + i2[0:1])
        cand_s = jnp.concatenate(cs, axis=0)
        cand_i = jnp.concatenate(ci, axis=0)
        top_s, pos = _topk_rows(cand_s, k)
        prow = lax.broadcasted_iota(I32, cand_s.shape, 0).astype(F32)
        sel = [jnp.sum(jnp.where(prow == pos[r:r + 1], cand_i, 0.0), axis=0, keepdims=True) for r in range(k)]
        top_i = jnp.concatenate(sel, axis=0)
        ex = jnp.exp(top_s - top_s[0:1])
        gate_rows.append(ex / jnp.sum(ex, axis=0, keepdims=True))
        idx_rows.append(top_i)
    idx_t = jnp.concatenate(idx_rows, axis=0)
    gate_t = jnp.concatenate([gate_rows[hh][r:r + 1] for r in range(k) for hh in range(n_heads)], axis=0)
    idx_ref[...] = (idx_t * 4.0).T.astype(I32)
    gate_ref[...] = gate_t.T


def _route_call(x1, mod8, g2, wq, keys, *, tb):
    b, s, d = x1.shape
    n_heads = keys.shape[0]
    nsel = n_heads * PEER_TOPK
    nblk = s // tb
    kern = functools.partial(_route_kernel, n_heads=n_heads)
    return pl.pallas_call(
        kern,
        grid=(b, nblk),
        in_specs=[pl.BlockSpec((None, tb, d), lambda i, j: (i, j, 0)),
                  pl.BlockSpec((None, 8, d), lambda i, j: (i, 0, 0)),
                  _const_spec(g2.shape), _const_spec(wq.shape), _const_spec(keys.shape)],
        out_specs=[pl.BlockSpec((tb * SUBLANES, LANES), lambda i, j: (i * nblk + j, 0)),
                   pl.BlockSpec((tb, nsel), lambda i, j: (i * nblk + j, 0)),
                   pl.BlockSpec((tb, nsel), lambda i, j: (i * nblk + j, 0))],
        out_shape=[jax.ShapeDtypeStruct((b * s * SUBLANES, LANES), F32),
                   jax.ShapeDtypeStruct((b * s, nsel), I32),
                   jax.ShapeDtypeStruct((b * s, nsel), F32)],
        compiler_params=pltpu.CompilerParams(dimension_semantics=("arbitrary", "arbitrary"),
                                             vmem_limit_bytes=VMEM_LIMIT_MIX),
        name="peer_route",
    )(x1, mod8, g2, wq, keys)


def _pack_kernel(x_ref, o_ref, stage):
    r = x_ref.shape[0]
    for c in range(SUBLANES):
        stage[pl.ds(c, r, stride=SUBLANES), :] = x_ref[:, c * LANES:(c + 1) * LANES]
    o_ref[...] = pltpu.bitcast(stage[...].astype(BF16), I32)


def _pack_table(tab, *, rows=512):
    n, d = tab.shape
    assert d == SUBLANES * LANES and n % rows == 0
    return pl.pallas_call(
        _pack_kernel,
        grid=(n // rows,),
        in_specs=[pl.BlockSpec((rows, d), lambda i: (i, 0))],
        out_specs=pl.BlockSpec((rows * 4, LANES), lambda i: (i, 0)),
        out_shape=jax.ShapeDtypeStruct((n * 4, LANES), I32),
        scratch_shapes=[pltpu.VMEM((rows * SUBLANES, LANES), F32)],
        name="peer_table_pack",
    )(tab)


def _gather_tile(tab_ref, ibuf, slot, positions):
    pieces = [tab_ref[pl.ds(pl.multiple_of(ibuf[slot, p], 4), 4), :] for p in positions]
    return pltpu.bitcast(jnp.concatenate(pieces, axis=0), BF16)


def _stage_ids(idx_ref, ibuf, sem, first_token, slot, nsel):
    n = IDX_GROUP * nsel
    return pltpu.make_async_copy(idx_ref.at[pl.ds(first_token * nsel, n)], ibuf.at[slot], sem.at[slot])


def _staged_token_loop(idx_ref, ibuf, sem, tbp, nsel, open_trip):
    ngroups = tbp // IDX_GROUP

    def stage(g, slot):
        return _stage_ids(idx_ref, ibuf, sem, g * IDX_GROUP, slot, nsel)

    for s in range(IDX_SLOTS - 1):
        stage(s, s).start()

    def trip(i, carry):
        per_token = open_trip(i)
        for s in range(IDX_SLOTS):
            g = i * IDX_SLOTS + s
            stage(g, s).wait()
            stage(jnp.minimum(g + IDX_SLOTS - 1, ngroups - 1), (s + IDX_SLOTS - 1) % IDX_SLOTS).start()
            for u in range(IDX_GROUP):
                per_token(s * IDX_GROUP + u, s, u)
        return carry

    lax.fori_loop(0, ngroups // IDX_SLOTS, trip, 0)
    for s in range(IDX_SLOTS - 1):
        stage(ngroups - 1, s).wait()


def _window(ref, first, n):
    return ref.at[pl.ds(pl.multiple_of(first, SUBLANES), n)]


def _split_bf16(x, parts):
    out = []
    for _ in range(parts - 1):
        p = x.astype(BF16)
        out.append(p)
        x = x - p.astype(F32)
    out.append(x.astype(BF16))
    return out


def _sel_order(q):
    return (q % SUBLANES) * PEER_TOPK + q // SUBLANES


def _peer_u_kernel(idx_ref, h_ref, gate_ref, tab_ref, act_ref, ibuf, sem, f_s, *, tbp, nsel):
    sub = lax.broadcasted_iota(I32, (SUBLANES, LANES), 0)
    lane = lax.broadcasted_iota(I32, (SUBLANES, LANES), 1)
    diag = sub == (lane & (SUBLANES - 1))
    lane0 = (lane & (SUBLANES - 1)) == 0
    m4 = (sub & 4) == 0
    m2 = (sub & 2) == 0
    m1 = (sub & 1) == 0

    def fold(a, b, dist, mask):
        return jnp.where(mask, a + pltpu.roll(a, SUBLANES - dist, 0), b + pltpu.roll(b, dist, 0))

    def column_sums(p):
        a0, a1 = fold(p[0], p[4], 4, m4), fold(p[2], p[6], 4, m4)
        a2, a3 = fold(p[1], p[5], 4, m4), fold(p[3], p[7], 4, m4)
        return fold(fold(a0, a1, 2, m2), fold(a2, a3, 2, m2), 1, m1)

    def partial_scores(h_win, tt, slot, u):
        hv = h_win[tt * SUBLANES:(tt + 1) * SUBLANES, :]
        h16 = jnp.concatenate(_split_bf16(hv, 2), axis=0)
        z = []
        for j in range(nsel // SEL_TILE):
            m = _gather_tile(tab_ref, ibuf, slot, [u * nsel + j * SEL_TILE + i for i in range(SEL_TILE)])
            o = lax.dot_general(h16, m, (((1,), (1,)), ((), ())), preferred_element_type=F32)
            o = o[0:SUBLANES] + o[SUBLANES:2 * SUBLANES]
            z += [jnp.where(diag, o[:, c * LANES:(c + 1) * LANES], 0.0) for c in range(SEL_TILE * SUBLANES // LANES)]
        f_s[tt] = column_sums(z)

    def finish_trip(act_win, gate_win):
        out = []
        for grp in range(TRIP // SUBLANES):
            g = []
            for j in range(SUBLANES):
                f = f_s[grp * SUBLANES + j]
                for sh in (4, 2, 1):
                    f = f + pltpu.roll(f, LANES - sh, 1)
                f = jnp.where(lane0, f, 0.0)
                g.append(pltpu.roll(f, 0, 1, stride=1, stride_axis=0))
            out.append(column_sums(g))
        act_win[...] = _gelu(jnp.concatenate(out, axis=0)) * gate_win[...]

    f_s[...] = jnp.zeros_like(f_s)

    def open_trip(i):
        tok0 = i * TRIP
        prev0 = jnp.maximum(tok0 - TRIP, 0)
        finish_trip(_window(act_ref, prev0, TRIP), _window(gate_ref, prev0, TRIP))
        h_win = _window(h_ref, tok0 * SUBLANES, TRIP * SUBLANES)
        return functools.partial(partial_scores, h_win)

    _staged_token_loop(idx_ref, ibuf, sem, tbp, nsel, open_trip)
    finish_trip(_window(act_ref, tbp - TRIP, TRIP), _window(gate_ref, tbp - TRIP, TRIP))


def _peer_v_kernel(idx_ref, act_ref, tab_ref, rep_ref, y_ref, ibuf, sem, arep_s, *, tbp, nsel):
    width = nsel * SUBLANES
    diag = (lax.broadcasted_iota(I32, (SUBLANES, width), 0)
            == (lax.broadcasted_iota(I32, (SUBLANES, width), 1) & (SUBLANES - 1)))

    def open_trip(i):
        tok0 = i * TRIP
        parts = jnp.concatenate(_split_bf16(act_ref[pl.ds(pl.multiple_of(tok0, TRIP), TRIP), :], ACT_PARTS), axis=0)
        arep_s[...] = jnp.dot(parts, rep_ref[...], preferred_element_type=F32)
        y_win = _window(y_ref, tok0 * SUBLANES, TRIP * SUBLANES)

        def per_token(tt, slot, u):
            lhs = jnp.concatenate(
                [jnp.where(diag, jnp.broadcast_to(arep_s[p * TRIP + tt:p * TRIP + tt + 1, :], (SUBLANES, width)), 0.0)
                 for p in range(ACT_PARTS)], axis=0).astype(BF16)
            acc = None
            for j in range(nsel // SEL_TILE):
                m = _gather_tile(tab_ref, ibuf, slot,
                                 [u * nsel + _sel_order(j * SEL_TILE + i) for i in range(SEL_TILE)])
                d = jnp.dot(lhs[:, j * SEL_TILE * SUBLANES:(j + 1) * SEL_TILE * SUBLANES], m,
                            preferred_element_type=F32)
                acc = d if acc is None else acc + d
            y = acc[0:SUBLANES]
            for p in range(1, ACT_PARTS):
                y = y + acc[p * SUBLANES:(p + 1) * SUBLANES]
            y_win[tt * SUBLANES:(tt + 1) * SUBLANES, :] = y

        return per_token

    _staged_token_loop(idx_ref, ibuf, sem, tbp, nsel, open_trip)


def _table_spec(shape):
    return pl.BlockSpec(shape, lambda i: (0, 0), pipeline_mode=pl.Buffered(1))


def _stage_scratch(nsel):
    return [pltpu.SMEM((IDX_SLOTS, IDX_GROUP * nsel), I32), pltpu.SemaphoreType.DMA((IDX_SLOTS,))]


def _peer_u_call(idx_flat, h8, gate, tab, *, tbp):
    t, nsel = gate.shape
    kern = functools.partial(_peer_u_kernel, tbp=tbp, nsel=nsel)
    return pl.pallas_call(
        kern,
        grid=(t // tbp,),
        in_specs=[pl.BlockSpec((tbp * nsel,), lambda i: (i,)),
                  pl.BlockSpec((tbp * SUBLANES, LANES), lambda i: (i, 0)),
                  pl.BlockSpec((tbp, nsel), lambda i: (i, 0)),
                  _table_spec(tab.shape)],
        out_specs=pl.BlockSpec((tbp, nsel), lambda i: (i, 0)),
        out_shape=jax.ShapeDtypeStruct((t, nsel), F32),
        scratch_shapes=_stage_scratch(nsel) + [pltpu.VMEM((TRIP, SUBLANES, LANES), F32)],
        compiler_params=pltpu.CompilerParams(dimension_semantics=("arbitrary",),
                                             vmem_limit_bytes=VMEM_LIMIT_PEER),
        name="peer_expert_scores",
    )(idx_flat, h8, gate, tab)


def _peer_v_call(idx_flat, act, tab, *, tbp):
    t, nsel = act.shape
    kern = functools.partial(_peer_v_kernel, tbp=tbp, nsel=nsel)
    rep = jnp.repeat(jnp.eye(nsel, dtype=BF16), SUBLANES, axis=1)
    return pl.pallas_call(
        kern,
        grid=(t // tbp,),
        in_specs=[pl.BlockSpec((tbp * nsel,), lambda i: (i,)),
                  pl.BlockSpec((tbp, nsel), lambda i: (i, 0)),
                  _table_spec(tab.shape),
                  pl.BlockSpec(rep.shape, lambda i: (0, 0))],
        out_specs=pl.BlockSpec((tbp * SUBLANES, LANES), lambda i: (i, 0)),
        out_shape=jax.ShapeDtypeStruct((t * SUBLANES, LANES), F32),
        scratch_shapes=_stage_scratch(nsel) + [pltpu.VMEM((ACT_PARTS * TRIP, nsel * SUBLANES), F32)],
        compiler_params=pltpu.CompilerParams(dimension_semantics=("arbitrary",),
                                             vmem_limit_bytes=VMEM_LIMIT_PEER),
        name="peer_expert_mix",
    )(idx_flat, act, tab, rep)


def _resid_kernel(x_ref, y_ref, mod_ref, g_ref, o_ref, *, final):
    tb = x_ref.shape[0]
    y = jnp.concatenate([y_ref[pl.ds(c, tb, stride=SUBLANES), :] for c in range(SUBLANES)], axis=-1)
    z = x_ref[...] + mod_ref[5:6, :] * y
    o_ref[...] = _rms(z) * g_ref[...] if final else z


def _resid_call(x1, y, mod8, g, *, tb, final):
    b, s, d = x1.shape
    nblk = s // tb
    return pl.pallas_call(
        functools.partial(_resid_kernel, final=final),
        grid=(b, nblk),
        in_specs=[pl.BlockSpec((None, tb, d), lambda i, j: (i, j, 0)),
                  pl.BlockSpec((tb * SUBLANES, LANES), lambda i, j: (i * nblk + j, 0)),
                  pl.BlockSpec((None, 8, d), lambda i, j: (i, 0, 0)),
                  _const_spec(g.shape)],
        out_specs=pl.BlockSpec((None, tb, d), lambda i, j: (i, j, 0)),
        out_shape=jax.ShapeDtypeStruct((b, s, d), F32),
        name="peer_residual_norm",
    )(x1, y, mod8, g)


def _pick_block(n, pref):
    while n % pref:
        pref //= 2
    return pref


def kernel(x, c, ada_w, ada_b, norm1_g, w_in, lb_gamma, hgrn_norm_g, gmlp_ln_g, gmlp_ln_b, spatial_w,
           spatial_b, gmlp_norm_g, w_out, norm2_g, peer_wq, peer_keys, peer_u, peer_v, final_g):
    b, s, d = x.shape
    depth = ada_w.shape[0]
    t = b * s
    lower_bounds = jnp.cumsum(jax.nn.softmax(lb_gamma.astype(F32), axis=0), axis=0)
    c_pad = jnp.zeros((8, d), F32).at[:b].set(c)
    tb_mix = _pick_block(s, 256)
    tb_route = _pick_block(s, 256)
    tb_res = _pick_block(s, 512)
    tbp = _pick_block(t, 512)
    assert tbp % TRIP == 0
    for l in range(depth):
        mod = _mod_call(c_pad, ada_w[l], ada_b[l])[:b]
        mod8 = jnp.concatenate([mod.reshape(b, 6, d), jnp.zeros((b, 2, d), F32)], axis=1)
        x1 = _mix_call(x, mod8, norm1_g[l][None], w_in[l].astype(BF16), lower_bounds[l][None],
                       hgrn_norm_g[l][None], gmlp_ln_g[l][None], gmlp_ln_b[l][None], gmlp_norm_g[l][None],
                       spatial_w[l], spatial_b[l].T, w_out[l].astype(BF16), tb=tb_mix)
        h2, idx4, gate = _route_call(x1, mod8, norm2_g[l][None], peer_wq[l].astype(BF16), peer_keys[l],
                                     tb=tb_route)
        idx_flat = idx4.reshape(-1)
        act = _peer_u_call(idx_flat, h2, gate, _pack_table(peer_u[l]), tbp=tbp)
        y8 = _peer_v_call(idx_flat, act, _pack_table(peer_v[l]), tbp=tbp)
        last = l == depth - 1
        x = _resid_call(x1, y8, mod8, final_g[None] if last else jnp.ones((1, d), F32),
                        tb=tb_res, final=last)
    return x
```

```python
import functools

import jax
import jax.numpy as jnp
from jax import lax
from jax.experimental import pallas as pl
from jax.experimental.pallas import tpu as pltpu

F32 = jnp.float32
BF16 = jnp.bfloat16
I32 = jnp.int32
HIGHEST = lax.Precision.HIGHEST
NORM_EPS = 1e-6

LANES = 128
SUBLANES = 8
HEAD_DIM = 128
GMLP_CHUNK = 128
SUB = 16
PEER_TOPK = 16
N_KEYS = 128
IDX_GROUP = 32
IDX_SLOTS = 2
TRIP = IDX_GROUP * IDX_SLOTS
SEL_TILE = 32
ACT_PARTS = 3
VMEM_LIMIT_MIX = 48 * 1024 * 1024
VMEM_LIMIT_PEER = 52 * 1024 * 1024


def _gelu(x):
    return 0.5 * x * (1.0 + jnp.tanh(0.7978845608028654 * (x + 0.044715 * (x * x * x))))


def _sigmoid(x):
    return 1.0 / (1.0 + jnp.exp(-x))


def _rms(x, eps=NORM_EPS):
    return x * lax.rsqrt(jnp.mean(x * x, axis=-1, keepdims=True) + eps)


def _const_spec(shape):
    nd = len(shape)
    return pl.BlockSpec(shape, lambda *_: (0,) * nd)


def _mod_kernel(c_ref, w_ref, b_ref, o_ref):
    c = c_ref[...]
    ca = c * _sigmoid(c)
    o_ref[...] = jnp.dot(ca, w_ref[...], precision=HIGHEST, preferred_element_type=F32) + b_ref[...]


def _mod_call(c_pad, w, b):
    rows, d = c_pad.shape
    n = w.shape[1]
    bn = 1536 if n % 1536 == 0 else n
    return pl.pallas_call(
        _mod_kernel,
        grid=(n // bn,),
        in_specs=[pl.BlockSpec((rows, d), lambda j: (0, 0)),
                  pl.BlockSpec((d, bn), lambda j: (0, j)),
                  pl.BlockSpec((1, bn), lambda j: (0, j))],
        out_specs=pl.BlockSpec((rows, bn), lambda j: (0, j)),
        out_shape=jax.ShapeDtypeStruct((rows, n), F32),
        name="adaln_mod",
    )(c_pad, w, b.reshape(1, n))


def _mix_kernel(x_ref, mod_ref, g1_ref, win_ref, lb_ref, hg_ref, lng_ref, lnb_ref, gng_ref,
                sw_ref, sbt_ref, wout_ref, o_ref,
                state, qe_s, ke_s, qf_s, kg_s, bb_s, vv_s, cat_s, *, tb, hw, gw):
    n_heads = hw // HEAD_DIM
    n_groups = gw // HEAD_DIM

    @pl.when(pl.program_id(1) == 0)
    def _():
        state[...] = jnp.zeros_like(state)

    x = x_ref[...]
    shift, scale, gate = mod_ref[0:1, :], mod_ref[1:2, :], mod_ref[2:3, :]
    h = (_rms(x) * g1_ref[...]) * (1.0 + scale) + shift
    proj = jnp.dot(h.astype(BF16), win_ref[...], preferred_element_type=F32)

    lb = lb_ref[...]
    q = proj[:, 0:hw]
    fz = proj[:, hw:2 * hw]
    e = jnp.exp(-jnp.abs(fz))
    r = 1.0 / (1.0 + e)
    pos = fz >= 0.0
    sig = jnp.where(pos, r, e * r)
    sig_neg = jnp.where(pos, e * r, r)
    logf = jnp.log(lb + (1.0 - lb) * sig)
    kg = (1.0 - lb) * sig_neg
    qf = q * _sigmoid(q)
    ri = lax.broadcasted_iota(I32, (tb, tb), 0)
    ci = lax.broadcasted_iota(I32, (tb, tb), 1)
    same = (ri // SUB) == (ci // SUB)
    tri_rows = jnp.where(same & (ci <= ri), 1.0, 0.0).astype(BF16)
    tri = jnp.concatenate([tri_rows, jnp.where(same & (ci > ri), 1.0, 0.0).astype(BF16)], axis=0)
    sums = sum(jnp.dot(tri, part, preferred_element_type=F32) for part in _split_bf16(logf, 3))
    bb = sums[0:tb]
    rem = sums[tb:2 * tb]
    qe_s[...] = qf * jnp.exp(bb)
    ke_s[...] = kg * jnp.exp(rem)
    qf_s[...] = qf
    kg_s[...] = kg
    bb_s[...] = bb
    vv_s[...] = proj[:, 2 * hw:3 * hw]

    rows16 = lax.broadcasted_iota(I32, (SUB, HEAD_DIM), 0)
    rows128 = lax.broadcasted_iota(I32, (GMLP_CHUNK, HEAD_DIM), 0)
    steps_per_chunk = GMLP_CHUNK // SUB

    def chunk_body(c, carry):
        c0 = pl.multiple_of(c * GMLP_CHUNK, GMLP_CHUNK)
        for hd in range(n_heads):
            cs = slice(hd * HEAD_DIM, (hd + 1) * HEAD_DIM)
            qe_c = qe_s[pl.ds(c0, GMLP_CHUNK), cs]
            ke_c = ke_s[pl.ds(c0, GMLP_CHUNK), cs]
            qf_c = qf_s[pl.ds(c0, GMLP_CHUNK), cs]
            kg_c = kg_s[pl.ds(c0, GMLP_CHUNK), cs]
            bb_c = bb_s[pl.ds(c0, GMLP_CHUNK), cs]
            v_c = vv_s[pl.ds(c0, GMLP_CHUNK), cs]
            v_t = v_c.T.astype(BF16)
            st = state[hd]
            outs = []
            for j in range(steps_per_chunk):
                rs = slice(j * SUB, (j + 1) * SUB)
                bb_b, qf_b, kg_b, v_b = bb_c[rs], qf_c[rs], kg_c[rs], v_c[rs]
                o = lax.dot_general(qe_c[rs].astype(BF16), st.astype(BF16), (((1,), (1,)), ((), ())),
                                    preferred_element_type=F32)
                for s in range(SUB):
                    dec = jnp.where(rows16 >= s, jnp.exp(bb_b - bb_b[s:s + 1, :]), 0.0)
                    w = jnp.sum(qf_b * dec * kg_b[s:s + 1, :], axis=-1, keepdims=True)
                    o = o + w * v_b[s:s + 1, :]
                outs.append(o)
                in_step = (rows128 >= j * SUB) & (rows128 < (j + 1) * SUB)
                ke_m = jnp.where(in_step, ke_c, 0.0).astype(BF16)
                upd = jnp.dot(v_t, ke_m, preferred_element_type=F32)
                st = jnp.exp(bb_b[SUB - 1:SUB, :]) * st + upd
            state[hd] = st
            o_c = jnp.concatenate(outs, axis=0)
            cat_s[pl.ds(c0, GMLP_CHUNK), cs] = _rms(o_c) * hg_ref[:, cs]
        return carry

    lax.fori_loop(0, tb // GMLP_CHUNK, chunk_body, 0)

    og = proj[:, 3 * hw:4 * hw]
    o_all = cat_s[:, 0:hw] * (og * _sigmoid(og))

    u = _gelu(proj[:, 4 * hw:4 * hw + gw])
    v = _gelu(proj[:, 4 * hw + gw:4 * hw + 2 * gw])
    mu = jnp.mean(v, axis=-1, keepdims=True)
    vc = v - mu
    var = jnp.mean(vc * vc, axis=-1, keepdims=True)
    vln = (vc * lax.rsqrt(var + NORM_EPS)) * lng_ref[...] + lnb_ref[...]
    tri = (lax.broadcasted_iota(I32, (GMLP_CHUNK, GMLP_CHUNK), 1)
           <= lax.broadcasted_iota(I32, (GMLP_CHUNK, GMLP_CHUNK), 0))
    gm_groups = []
    for g in range(n_groups):
        gs = slice(g * HEAD_DIM, (g + 1) * HEAD_DIM)
        wm = jnp.where(tri, sw_ref[g], 0.0).astype(BF16)
        bias = sbt_ref[:, g:g + 1]
        parts = []
        for cidx in range(tb // GMLP_CHUNK):
            rs = slice(cidx * GMLP_CHUNK, (cidx + 1) * GMLP_CHUNK)
            mixed = jnp.dot(wm, vln[rs, gs].astype(BF16), preferred_element_type=F32) + bias
            parts.append(u[rs, gs] * mixed)
        gmg = jnp.concatenate(parts, axis=0) if len(parts) > 1 else parts[0]
        gm_groups.append(_rms(gmg) * gng_ref[:, gs])
    gm = jnp.concatenate(gm_groups, axis=-1)

    cat = jnp.concatenate([o_all, gm], axis=-1).astype(BF16)
    mixed_out = jnp.dot(cat, wout_ref[...], preferred_element_type=F32)
    o_ref[...] = x + gate * mixed_out


def _mix_call(x, mod8, g1, w_in, lb, hg, lng, lnb, gng, sw, sbt, w_out, *, tb):
    b, s, d = x.shape
    hw = lb.shape[1]
    gw = lng.shape[1]
    n_heads = hw // HEAD_DIM
    kern = functools.partial(_mix_kernel, tb=tb, hw=hw, gw=gw)
    return pl.pallas_call(
        kern,
        grid=(b, s // tb),
        in_specs=[pl.BlockSpec((None, tb, d), lambda i, j: (i, j, 0)),
                  pl.BlockSpec((None, 8, d), lambda i, j: (i, 0, 0)),
                  _const_spec(g1.shape), _const_spec(w_in.shape), _const_spec(lb.shape),
                  _const_spec(hg.shape), _const_spec(lng.shape), _const_spec(lnb.shape),
                  _const_spec(gng.shape), _const_spec(sw.shape), _const_spec(sbt.shape),
                  _const_spec(w_out.shape)],
        out_specs=pl.BlockSpec((None, tb, d), lambda i, j: (i, j, 0)),
        out_shape=jax.ShapeDtypeStruct((b, s, d), F32),
        scratch_shapes=[pltpu.VMEM((n_heads, HEAD_DIM, HEAD_DIM), F32)]
                       + [pltpu.VMEM((tb, hw), F32) for _ in range(7)],
        compiler_params=pltpu.CompilerParams(dimension_semantics=("arbitrary", "arbitrary"),
                                             vmem_limit_bytes=VMEM_LIMIT_MIX),
        name="hgrn_gmlp_mixer",
    )(x, mod8, g1, w_in, lb, hg, lng, lnb, gng, sw, sbt, w_out)


def _topk_rows(x, k):
    n = x.shape[0]
    rid = lax.broadcasted_iota(I32, x.shape, 0).astype(F32)
    vals, ids = [], []
    for _ in range(k):
        m = jnp.max(x, axis=0, keepdims=True)
        am = jnp.min(jnp.where(x == m, rid, float(n)), axis=0, keepdims=True)
        vals.append(m)
        ids.append(am)
        x = jnp.where(rid == am, -jnp.inf, x)
    return jnp.concatenate(vals, axis=0), jnp.concatenate(ids, axis=0)


def _route_kernel(x_ref, mod_ref, g2_ref, wq_ref, keys_ref, h_ref, idx_ref, gate_ref, *, n_heads):
    x = x_ref[...]
    shift, scale = mod_ref[3:4, :], mod_ref[4:5, :]
    h = (_rms(x) * g2_ref[...]) * (1.0 + scale) + shift
    tb = x.shape[0]
    for c in range(SUBLANES):
        h_ref[pl.ds(c, tb, stride=SUBLANES), :] = h[:, c * LANES:(c + 1) * LANES]
    q = jnp.dot(h.astype(BF16), wq_ref[...], preferred_element_type=F32)
    k = PEER_TOPK
    jrow8 = lax.broadcasted_iota(I32, (SUBLANES, tb), 0)
    idx_rows, gate_rows = [], []
    for hh in range(n_heads):
        tops = []
        for p in range(2):
            col = (hh * 2 + p) * HEAD_DIM
            sim_t = lax.dot_general(keys_ref[hh, p].astype(BF16), q[:, col:col + HEAD_DIM].astype(BF16),
                                    (((1,), (1,)), ((), ())), preferred_element_type=F32)
            tops.append(_topk_rows(sim_t, k))
        (s1, i1), (s2, i2) = tops
        cs = [s1[0:1] + s2]
        ci = [i1[0:1] * N_KEYS + i2]
        for i in range(1, SUBLANES):
            nvalid = k // (i + 1)
            cs.append(jnp.where(jrow8 < nvalid, s1[i:i + 1] + s2[0:SUBLANES], -jnp.inf))
            ci.append(i1[i:i + 1] * N_KEYS + i2[0:SUBLANES])
        cs.append(s1[SUBLANES:k] + s2[0:1])
        ci.append(i1[SUBLANES:k] * N_KEYS + i2[0:1])
        cand_s = jnp.concatenate(cs, axis=0)
        cand_i = jnp.concatenate(ci, axis=0)
        top_s, pos = _topk_rows(cand_s, k)
        prow = lax.broadcasted_iota(I32, cand_s.shape, 0).astype(F32)
        sel = [jnp.sum(jnp.where(prow == pos[r:r + 1], cand_i, 0.0), axis=0, keepdims=True) for r in range(k)]
        top_i = jnp.concatenate(sel, axis=0)
        ex = jnp.exp(top_s - top_s[0:1])
        gate_rows.append(ex / jnp.sum(ex, axis=0, keepdims=True))
        idx_rows.append(top_i)
    idx_t = jnp.concatenate(idx_rows, axis=0)
    gate_t = jnp.concatenate([gate_rows[hh][r:r + 1] for r in range(k) for hh in range(n_heads)], axis=0)
    idx_ref[...] = (idx_t * 4.0).T.astype(I32)
    gate_ref[...] = gate_t.T


def _route_call(x1, mod8, g2, wq, keys, *, tb):
    b, s, d = x1.shape
    n_heads = keys.shape[0]
    nsel = n_heads * PEER_TOPK
    nblk = s // tb
    kern = functools.partial(_route_kernel, n_heads=n_heads)
    return pl.pallas_call(
        kern,
        grid=(b, nblk),
        in_specs=[pl.BlockSpec((None, tb, d), lambda i, j: (i, j, 0)),
                  pl.BlockSpec((None, 8, d), lambda i, j: (i, 0, 0)),
                  _const_spec(g2.shape), _const_spec(wq.shape), _const_spec(keys.shape)],
        out_specs=[pl.BlockSpec((tb * SUBLANES, LANES), lambda i, j: (i * nblk + j, 0)),
                   pl.BlockSpec((tb, nsel), lambda i, j: (i * nblk + j, 0)),
                   pl.BlockSpec((tb, nsel), lambda i, j: (i * nblk + j, 0))],
        out_shape=[jax.ShapeDtypeStruct((b * s * SUBLANES, LANES), F32),
                   jax.ShapeDtypeStruct((b * s, nsel), I32),
                   jax.ShapeDtypeStruct((b * s, nsel), F32)],
        compiler_params=pltpu.CompilerParams(dimension_semantics=("arbitrary", "arbitrary"),
                                             vmem_limit_bytes=VMEM_LIMIT_MIX),
        name="peer_route",
    )(x1, mod8, g2, wq, keys)


def _pack_kernel(x_ref, o_ref, stage):
    r = x_ref.shape[0]
    for c in range(SUBLANES):
        stage[pl.ds(c, r, stride=SUBLANES), :] = x_ref[:, c * LANES:(c + 1) * LANES]
    o_ref[...] = pltpu.bitcast(stage[...].astype(BF16), I32)


def _pack_table(tab, *, rows=512):
    n, d = tab.shape
    assert d == SUBLANES * LANES and n % rows == 0
    return pl.pallas_call(
        _pack_kernel,
        grid=(n // rows,),
        in_specs=[pl.BlockSpec((rows, d), lambda i: (i, 0))],
        out_specs=pl.BlockSpec((rows * 4, LANES), lambda i: (i, 0)),
        out_shape=jax.ShapeDtypeStruct((n * 4, LANES), I32),
        scratch_shapes=[pltpu.VMEM((rows * SUBLANES, LANES), F32)],
        name="peer_table_pack",
    )(tab)


def _gather_tile(tab_ref, ibuf, slot, positions):
    pieces = [tab_ref[pl.ds(pl.multiple_of(ibuf[slot, p], 4), 4), :] for p in positions]
    return pltpu.bitcast(jnp.concatenate(pieces, axis=0), BF16)


def _stage_ids(idx_ref, ibuf, sem, first_token, slot, nsel):
    n = IDX_GROUP * nsel
    return pltpu.make_async_copy(idx_ref.at[pl.ds(first_token * nsel, n)], ibuf.at[slot], sem.at[slot])


def _staged_token_loop(idx_ref, ibuf, sem, tbp, nsel, open_trip):
    ngroups = tbp // IDX_GROUP

    def stage(g, slot):
        return _stage_ids(idx_ref, ibuf, sem, g * IDX_GROUP, slot, nsel)

    for s in range(IDX_SLOTS - 1):
        stage(s, s).start()

    def trip(i, carry):
        per_token = open_trip(i)
        for s in range(IDX_SLOTS):
            g = i * IDX_SLOTS + s
            stage(g, s).wait()
            stage(jnp.minimum(g + IDX_SLOTS - 1, ngroups - 1), (s + IDX_SLOTS - 1) % IDX_SLOTS).start()
            for u in range(IDX_GROUP):
                per_token(s * IDX_GROUP + u, s, u)
        return carry

    lax.fori_loop(0, ngroups // IDX_SLOTS, trip, 0)
    for s in range(IDX_SLOTS - 1):
        stage(ngroups - 1, s).wait()


def _window(ref, first, n):
    return ref.at[pl.ds(pl.multiple_of(first, SUBLANES), n)]


def _split_bf16(x, parts):
    out = []
    for _ in range(parts - 1):
        p = x.astype(BF16)
        out.append(p)
        x = x - p.astype(F32)
    out.append(x.astype(BF16))
    return out


def _sel_order(q):
    return (q % SUBLANES) * PEER_TOPK + q // SUBLANES


def _peer_u_kernel(idx_ref, h_ref, gate_ref, tab_ref, act_ref, ibuf, sem, f_s, *, tbp, nsel):
    sub = lax.broadcasted_iota(I32, (SUBLANES, LANES), 0)
    lane = lax.broadcasted_iota(I32, (SUBLANES, LANES), 1)
    diag = sub == (lane & (SUBLANES - 1))
    lane0 = (lane & (SUBLANES - 1)) == 0
    m4 = (sub & 4) == 0
    m2 = (sub & 2) == 0
    m1 = (sub & 1) == 0

    def fold(a, b, dist, mask):
        return jnp.where(mask, a + pltpu.roll(a, SUBLANES - dist, 0), b + pltpu.roll(b, dist, 0))

    def column_sums(p):
        a0, a1 = fold(p[0], p[4], 4, m4), fold(p[2], p[6], 4, m4)
        a2, a3 = fold(p[1], p[5], 4, m4), fold(p[3], p[7], 4, m4)
        return fold(fold(a0, a1, 2, m2), fold(a2, a3, 2, m2), 1, m1)

    def partial_scores(h_win, tt, slot, u):
        hv = h_win[tt * SUBLANES:(tt + 1) * SUBLANES, :]
        h16 = jnp.concatenate(_split_bf16(hv, 2), axis=0)
        z = []
        for j in range(nsel // SEL_TILE):
            m = _gather_tile(tab_ref, ibuf, slot, [u * nsel + j * SEL_TILE + i for i in range(SEL_TILE)])
            o = lax.dot_general(h16, m, (((1,), (1,)), ((), ())), preferred_element_type=F32)
            o = o[0:SUBLANES] + o[SUBLANES:2 * SUBLANES]
            z += [jnp.where(diag, o[:, c * LANES:(c + 1) * LANES], 0.0) for c in range(SEL_TILE * SUBLANES // LANES)]
        f_s[tt] = column_sums(z)

    def finish_trip(act_win, gate_win):
        out = []
        for grp in range(TRIP // SUBLANES):
            g = []
            for j in range(SUBLANES):
                f = f_s[grp * SUBLANES + j]
                for sh in (4, 2, 1):
                    f = f + pltpu.roll(f, LANES - sh, 1)
                f = jnp.where(lane0, f, 0.0)
                g.append(pltpu.roll(f, 0, 1, stride=1, stride_axis=0))
            out.append(column_sums(g))
        act_win[...] = _gelu(jnp.concatenate(out, axis=0)) * gate_win[...]

    f_s[...] = jnp.zeros_like(f_s)

    def open_trip(i):
        tok0 = i * TRIP
        prev0 = jnp.maximum(tok0 - TRIP, 0)
        finish_trip(_window(act_ref, prev0, TRIP), _window(gate_ref, prev0, TRIP))
        h_win = _window(h_ref, tok0 * SUBLANES, TRIP * SUBLANES)
        return functools.partial(partial_scores, h_win)

    _staged_token_loop(idx_ref, ibuf, sem, tbp, nsel, open_trip)
    finish_trip(_window(act_ref, tbp - TRIP, TRIP), _window(gate_ref, tbp - TRIP, TRIP))


def _peer_v_kernel(idx_ref, act_ref, tab_ref, rep_ref, y_ref, ibuf, sem, arep_s, *, tbp, nsel):
    width = nsel * SUBLANES
    diag = (lax.broadcasted_iota(I32, (SUBLANES, width), 0)
            == (lax.broadcasted_iota(I32, (SUBLANES, width), 1) & (SUBLANES - 1)))

    def open_trip(i):
        tok0 = i * TRIP
        parts = jnp.concatenate(_split_bf16(act_ref[pl.ds(pl.multiple_of(tok0, TRIP), TRIP), :], ACT_PARTS), axis=0)
        arep_s[...] = jnp.dot(parts, rep_ref[...], preferred_element_type=F32)
        y_win = _window(y_ref, tok0 * SUBLANES, TRIP * SUBLANES)

        def per_token(tt, slot, u):
            lhs = jnp.concatenate(
                [jnp.where(diag, jnp.broadcast_to(arep_s[p * TRIP + tt:p * TRIP + tt + 1, :], (SUBLANES, width)), 0.0)
                 for p in range(ACT_PARTS)], axis=0).astype(BF16)
            acc = None
            for j in range(nsel // SEL_TILE):
                m = _gather_tile(tab_ref, ibuf, slot,
                                 [u * nsel + _sel_order(j * SEL_TILE + i) for i in range(SEL_TILE)])
                d = jnp.dot(lhs[:, j * SEL_TILE * SUBLANES:(j + 1) * SEL_TILE * SUBLANES], m,
                            preferred_element_type=F32)
                acc = d if acc is None else acc + d
            y = acc[0:SUBLANES]
            for p in range(1, ACT_PARTS):
                y = y + acc[p * SUBLANES:(p + 1) * SUBLANES]
            y_win[tt * SUBLANES:(tt + 1) * SUBLANES, :] = y

        return per_token

    _staged_token_loop(idx_ref, ibuf, sem, tbp, nsel, open_trip)


def _table_spec(shape):
    return pl.BlockSpec(shape, lambda i: (0, 0), pipeline_mode=pl.Buffered(1))


def _stage_scratch(nsel):
    return [pltpu.SMEM((IDX_SLOTS, IDX_GROUP * nsel), I32), pltpu.SemaphoreType.DMA((IDX_SLOTS,))]


def _peer_u_call(idx_flat, h8, gate, tab, *, tbp):
    t, nsel = gate.shape
    kern = functools.partial(_peer_u_kernel, tbp=tbp, nsel=nsel)
    return pl.pallas_call(
        kern,
        grid=(t // tbp,),
        in_specs=[pl.BlockSpec((tbp * nsel,), lambda i: (i,)),
                  pl.BlockSpec((tbp * SUBLANES, LANES), lambda i: (i, 0)),
                  pl.BlockSpec((tbp, nsel), lambda i: (i, 0)),
                  _table_spec(tab.shape)],
        out_specs=pl.BlockSpec((tbp, nsel), lambda i: (i, 0)),
        out_shape=jax.ShapeDtypeStruct((t, nsel), F32),
        scratch_shapes=_stage_scratch(nsel) + [pltpu.VMEM((TRIP, SUBLANES, LANES), F32)],
        compiler_params=pltpu.CompilerParams(dimension_semantics=("arbitrary",),
                                             vmem_limit_bytes=VMEM_LIMIT_PEER),
        name="peer_expert_scores",
    )(idx_flat, h8, gate, tab)


def _peer_v_call(idx_flat, act, tab, *, tbp):
    t, nsel = act.shape
    kern = functools.partial(_peer_v_kernel, tbp=tbp, nsel=nsel)
    rep = jnp.repeat(jnp.eye(nsel, dtype=BF16), SUBLANES, axis=1)
    return pl.pallas_call(
        kern,
        grid=(t // tbp,),
        in_specs=[pl.BlockSpec((tbp * nsel,), lambda i: (i,)),
                  pl.BlockSpec((tbp, nsel), lambda i: (i, 0)),
                  _table_spec(tab.shape),
                  pl.BlockSpec(rep.shape, lambda i: (0, 0))],
        out_specs=pl.BlockSpec((tbp * SUBLANES, LANES), lambda i: (i, 0)),
        out_shape=jax.ShapeDtypeStruct((t * SUBLANES, LANES), F32),
        scratch_shapes=_stage_scratch(nsel) + [pltpu.VMEM((ACT_PARTS * TRIP, nsel * SUBLANES), F32)],
        compiler_params=pltpu.CompilerParams(dimension_semantics=("arbitrary",),
                                             vmem_limit_bytes=VMEM_LIMIT_PEER),
        name="peer_expert_mix",
    )(idx_flat, act, tab, rep)


def _resid_kernel(x_ref, y_ref, mod_ref, g_ref, o_ref, *, final):
    tb = x_ref.shape[0]
    y = jnp.concatenate([y_ref[pl.ds(c, tb, stride=SUBLANES), :] for c in range(SUBLANES)], axis=-1)
    z = x_ref[...] + mod_ref[5:6, :] * y
    o_ref[...] = _rms(z) * g_ref[...] if final else z


def _resid_call(x1, y, mod8, g, *, tb, final):
    b, s, d = x1.shape
    nblk = s // tb
    return pl.pallas_call(
        functools.partial(_resid_kernel, final=final),
        grid=(b, nblk),
        in_specs=[pl.BlockSpec((None, tb, d), lambda i, j: (i, j, 0)),
                  pl.BlockSpec((tb * SUBLANES, LANES), lambda i, j: (i * nblk + j, 0)),
                  pl.BlockSpec((None, 8, d), lambda i, j: (i, 0, 0)),
                  _const_spec(g.shape)],
        out_specs=pl.BlockSpec((None, tb, d), lambda i, j: (i, j, 0)),
        out_shape=jax.ShapeDtypeStruct((b, s, d), F32),
        name="peer_residual_norm",
    )(x1, y, mod8, g)


def _pick_block(n, pref):
    while n % pref:
        pref //= 2
    return pref


def kernel(x, c, ada_w, ada_b, norm1_g, w_in, lb_gamma, hgrn_norm_g, gmlp_ln_g, gmlp_ln_b, spatial_w,
           spatial_b, gmlp_norm_g, w_out, norm2_g, peer_wq, peer_keys, peer_u, peer_v, final_g):
    b, s, d = x.shape
    depth = ada_w.shape[0]
    t = b * s
    lower_bounds = jnp.cumsum(jax.nn.softmax(lb_gamma.astype(F32), axis=0), axis=0)
    c_pad = jnp.zeros((8, d), F32).at[:b].set(c)
    tb_mix = _pick_block(s, 256)
    tb_route = _pick_block(s, 256)
    tb_res = _pick_block(s, 512)
    tbp = _pick_block(t, 512)
    assert tbp % TRIP == 0
    for l in range(depth):
        mod = _mod_call(c_pad, ada_w[l], ada_b[l])[:b]
        mod8 = jnp.concatenate([mod.reshape(b, 6, d), jnp.zeros((b, 2, d), F32)], axis=1)
        x1 = _mix_call(x, mod8, norm1_g[l][None], w_in[l].astype(BF16), lower_bounds[l][None],
                       hgrn_norm_g[l][None], gmlp_ln_g[l][None], gmlp_ln_b[l][None], gmlp_norm_g[l][None],
                       spatial_w[l], spatial_b[l].T, w_out[l].astype(BF16), tb=tb_mix)
        h2, idx4, gate = _route_call(x1, mod8, norm2_g[l][None], peer_wq[l].astype(BF16), peer_keys[l],
                                     tb=tb_route)
        idx_flat = idx4.reshape(-1)
        act = _peer_u_call(idx_flat, h2, gate, _pack_table(peer_u[l]), tbp=tbp)
        y8 = _peer_v_call(idx_flat, act, _pack_table(peer_v[l]), tbp=tbp)
        last = l == depth - 1
        x = _resid_call(x1, y8, mod8, final_g[None] if last else jnp.ones((1, d), F32),
                        tb=tb_res, final=last)
    return x
```

```python
import functools

import jax
import jax.numpy as jnp
from jax import lax
from jax.experimental import pallas as pl
from jax.experimental.pallas import tpu as pltpu

F32 = jnp.float32
BF16 = jnp.bfloat16
I32 = jnp.int32
HIGHEST = lax.Precision.HIGHEST
NORM_EPS = 1e-6

LANES = 128
SUBLANES = 8
HEAD_DIM = 128
GMLP_CHUNK = 128
SUB = 16
PEER_TOPK = 16
N_KEYS = 128
IDX_SLOTS = 2
U_GROUP = 32
V_GROUP = 64
SEL_TILE = 32
ACT_PARTS = 2
VMEM_LIMIT_MIX = 48 * 1024 * 1024
VMEM_LIMIT_PEER = 52 * 1024 * 1024


def _gelu(x):
    return 0.5 * x * (1.0 + jnp.tanh(0.7978845608028654 * (x + 0.044715 * (x * x * x))))


def _sigmoid(x):
    return 1.0 / (1.0 + jnp.exp(-x))


def _rms(x, eps=NORM_EPS):
    return x * lax.rsqrt(jnp.mean(x * x, axis=-1, keepdims=True) + eps)


def _const_spec(shape):
    nd = len(shape)
    return pl.BlockSpec(shape, lambda *_: (0,) * nd)


def _mod_kernel(c_ref, w_ref, b_ref, o_ref):
    c = c_ref[...]
    ca = c * _sigmoid(c)
    o_ref[...] = jnp.dot(ca, w_ref[...], precision=HIGHEST, preferred_element_type=F32) + b_ref[...]


def _mod_call(c_pad, w, b):
    rows, d = c_pad.shape
    n = w.shape[1]
    bn = 1536 if n % 1536 == 0 else n
    return pl.pallas_call(
        _mod_kernel,
        grid=(n // bn,),
        in_specs=[pl.BlockSpec((rows, d), lambda j: (0, 0)),
                  pl.BlockSpec((d, bn), lambda j: (0, j)),
                  pl.BlockSpec((1, bn), lambda j: (0, j))],
        out_specs=pl.BlockSpec((rows, bn), lambda j: (0, j)),
        out_shape=jax.ShapeDtypeStruct((rows, n), F32),
        name="adaln_mod",
    )(c_pad, w, b.reshape(1, n))


def _mix_kernel(x_ref, mod_ref, g1_ref, win_ref, lb_ref, hg_ref, lng_ref, lnb_ref, gng_ref,
                sw_ref, sbt_ref, wout_ref, o_ref,
                state, qe_s, ke_s, qf_s, kg_s, bb_s, vv_s, cat_s, *, tb, hw, gw):
    n_heads = hw // HEAD_DIM
    n_groups = gw // HEAD_DIM

    @pl.when(pl.program_id(1) == 0)
    def _():
        state[...] = jnp.zeros_like(state)

    x = x_ref[...]
    shift, scale, gate = mod_ref[0:1, :], mod_ref[1:2, :], mod_ref[2:3, :]
    h = (_rms(x) * g1_ref[...]) * (1.0 + scale) + shift
    proj = jnp.dot(h.astype(BF16), win_ref[...], preferred_element_type=F32)

    lb = lb_ref[...]
    q = proj[:, 0:hw]
    fz = proj[:, hw:2 * hw]
    e = jnp.exp(-jnp.abs(fz))
    r = 1.0 / (1.0 + e)
    pos = fz >= 0.0
    sig = jnp.where(pos, r, e * r)
    sig_neg = jnp.where(pos, e * r, r)
    logf = jnp.log(lb + (1.0 - lb) * sig)
    kg = (1.0 - lb) * sig_neg
    qf = q * _sigmoid(q)
    ri = lax.broadcasted_iota(I32, (tb, tb), 0)
    ci = lax.broadcasted_iota(I32, (tb, tb), 1)
    same = (ri // SUB) == (ci // SUB)
    tri_rows = jnp.where(same & (ci <= ri), 1.0, 0.0).astype(BF16)
    tri = jnp.concatenate([tri_rows, jnp.where(same & (ci > ri), 1.0, 0.0).astype(BF16)], axis=0)
    sums = sum(jnp.dot(tri, part, preferred_element_type=F32) for part in _split_bf16(logf, 3))
    bb = sums[0:tb]
    rem = sums[tb:2 * tb]
    qe_s[...] = qf * jnp.exp(bb)
    ke_s[...] = kg * jnp.exp(rem)
    qf_s[...] = qf
    kg_s[...] = kg
    bb_s[...] = bb
    vv_s[...] = proj[:, 2 * hw:3 * hw]

    rows16 = lax.broadcasted_iota(I32, (SUB, HEAD_DIM), 0)
    rows128 = lax.broadcasted_iota(I32, (GMLP_CHUNK, HEAD_DIM), 0)
    steps_per_chunk = GMLP_CHUNK // SUB

    def chunk_body(c, carry):
        c0 = pl.multiple_of(c * GMLP_CHUNK, GMLP_CHUNK)
        for hd in range(n_heads):
            cs = slice(hd * HEAD_DIM, (hd + 1) * HEAD_DIM)
            qe_c = qe_s[pl.ds(c0, GMLP_CHUNK), cs]
            ke_c = ke_s[pl.ds(c0, GMLP_CHUNK), cs]
            qf_c = qf_s[pl.ds(c0, GMLP_CHUNK), cs]
            kg_c = kg_s[pl.ds(c0, GMLP_CHUNK), cs]
            bb_c = bb_s[pl.ds(c0, GMLP_CHUNK), cs]
            v_c = vv_s[pl.ds(c0, GMLP_CHUNK), cs]
            v_t = v_c.T.astype(BF16)
            st = state[hd]
            outs = []
            for j in range(steps_per_chunk):
                rs = slice(j * SUB, (j + 1) * SUB)
                bb_b, qf_b, kg_b, v_b = bb_c[rs], qf_c[rs], kg_c[rs], v_c[rs]
                o = lax.dot_general(qe_c[rs].astype(BF16), st.astype(BF16), (((1,), (1,)), ((), ())),
                                    preferred_element_type=F32)
                for s in range(SUB):
                    dec = jnp.where(rows16 >= s, jnp.exp(bb_b - bb_b[s:s + 1, :]), 0.0)
                    w = jnp.sum(qf_b * dec * kg_b[s:s + 1, :], axis=-1, keepdims=True)
                    o = o + w * v_b[s:s + 1, :]
                outs.append(o)
                in_step = (rows128 >= j * SUB) & (rows128 < (j + 1) * SUB)
                ke_m = jnp.where(in_step, ke_c, 0.0).astype(BF16)
                upd = jnp.dot(v_t, ke_m, preferred_element_type=F32)
                st = jnp.exp(bb_b[SUB - 1:SUB, :]) * st + upd
            state[hd] = st
            o_c = jnp.concatenate(outs, axis=0)
            cat_s[pl.ds(c0, GMLP_CHUNK), cs] = _rms(o_c) * hg_ref[:, cs]
        return carry

    lax.fori_loop(0, tb // GMLP_CHUNK, chunk_body, 0)

    og = proj[:, 3 * hw:4 * hw]
    o_all = cat_s[:, 0:hw] * (og * _sigmoid(og))

    u = _gelu(proj[:, 4 * hw:4 * hw + gw])
    v = _gelu(proj[:, 4 * hw + gw:4 * hw + 2 * gw])
    mu = jnp.mean(v, axis=-1, keepdims=True)
    vc = v - mu
    var = jnp.mean(vc * vc, axis=-1, keepdims=True)
    vln = (vc * lax.rsqrt(var + NORM_EPS)) * lng_ref[...] + lnb_ref[...]
    tri = (lax.broadcasted_iota(I32, (GMLP_CHUNK, GMLP_CHUNK), 1)
           <= lax.broadcasted_iota(I32, (GMLP_CHUNK, GMLP_CHUNK), 0))
    gm_groups = []
    for g in range(n_groups):
        gs = slice(g * HEAD_DIM, (g + 1) * HEAD_DIM)
        wm = jnp.where(tri, sw_ref[g], 0.0).astype(BF16)
        bias = sbt_ref[:, g:g + 1]
        parts = []
        for cidx in range(tb // GMLP_CHUNK):
            rs = slice(cidx * GMLP_CHUNK, (cidx + 1) * GMLP_CHUNK)
            mixed = jnp.dot(wm, vln[rs, gs].astype(BF16), preferred_element_type=F32) + bias
            parts.append(u[rs, gs] * mixed)
        gmg = jnp.concatenate(parts, axis=0) if len(parts) > 1 else parts[0]
        gm_groups.append(_rms(gmg) * gng_ref[:, gs])
    gm = jnp.concatenate(gm_groups, axis=-1)

    cat = jnp.concatenate([o_all, gm], axis=-1).astype(BF16)
    mixed_out = jnp.dot(cat, wout_ref[...], preferred_element_type=F32)
    o_ref[...] = x + gate * mixed_out


def _mix_call(x, mod8, g1, w_in, lb, hg, lng, lnb, gng, sw, sbt, w_out, *, tb):
    b, s, d = x.shape
    hw = lb.shape[1]
    gw = lng.shape[1]
    n_heads = hw // HEAD_DIM
    kern = functools.partial(_mix_kernel, tb=tb, hw=hw, gw=gw)
    return pl.pallas_call(
        kern,
        grid=(b, s // tb),
        in_specs=[pl.BlockSpec((None, tb, d), lambda i, j: (i, j, 0)),
                  pl.BlockSpec((None, 8, d), lambda i, j: (i, 0, 0)),
                  _const_spec(g1.shape), _const_spec(w_in.shape), _const_spec(lb.shape),
                  _const_spec(hg.shape), _const_spec(lng.shape), _const_spec(lnb.shape),
                  _const_spec(gng.shape), _const_spec(sw.shape), _const_spec(sbt.shape),
                  _const_spec(w_out.shape)],
        out_specs=pl.BlockSpec((None, tb, d), lambda i, j: (i, j, 0)),
        out_shape=jax.ShapeDtypeStruct((b, s, d), F32),
        scratch_shapes=[pltpu.VMEM((n_heads, HEAD_DIM, HEAD_DIM), F32)]
                       + [pltpu.VMEM((tb, hw), F32) for _ in range(7)],
        compiler_params=pltpu.CompilerParams(dimension_semantics=("arbitrary", "arbitrary"),
                                             vmem_limit_bytes=VMEM_LIMIT_MIX),
        name="hgrn_gmlp_mixer",
    )(x, mod8, g1, w_in, lb, hg, lng, lnb, gng, sw, sbt, w_out)


def _topk_rows(x, k):
    n = x.shape[0]
    rid = lax.broadcasted_iota(I32, x.shape, 0).astype(F32)
    vals, ids = [], []
    for _ in range(k):
        m = jnp.max(x, axis=0, keepdims=True)
        am = jnp.min(jnp.where(x == m, rid, float(n)), axis=0, keepdims=True)
        vals.append(m)
        ids.append(am)
        x = jnp.where(rid == am, -jnp.inf, x)
    return jnp.concatenate(vals, axis=0), jnp.concatenate(ids, axis=0)


def _route_kernel(x_ref, mod_ref, g2_ref, wq_ref, keys_ref, h_ref, idx_ref, gate_ref, *, n_heads):
    x = x_ref[...]
    shift, scale = mod_ref[3:4, :], mod_ref[4:5, :]
    h = (_rms(x) * g2_ref[...]) * (1.0 + scale) + shift
    tb = x.shape[0]
    for c in range(SUBLANES):
        h_ref[pl.ds(c, tb, stride=SUBLANES), :] = h[:, c * LANES:(c + 1) * LANES]
    q = jnp.dot(h.astype(BF16), wq_ref[...], preferred_element_type=F32)
    k = PEER_TOPK
    jrow8 = lax.broadcasted_iota(I32, (SUBLANES, tb), 0)
    idx_rows, gate_rows = [], []
    for hh in range(n_heads):
        tops = []
        for p in range(2):
            col = (hh * 2 + p) * HEAD_DIM
            sim_t = lax.dot_general(keys_ref[hh, p].astype(BF16), q[:, col:col + HEAD_DIM].astype(BF16),
                                    (((1,), (1,)), ((), ())), preferred_element_type=F32)
            tops.append(_topk_rows(sim_t, k))
        (s1, i1), (s2, i2) = tops
        cs = [s1[0:1] + s2]
        ci = [i1[0:1] * N_KEYS + i2]
        for i in range(1, SUBLANES):
            nvalid = k // (i + 1)
            cs.append(jnp.where(jrow8 < nvalid, s1[i:i + 1] + s2[0:SUBLANES], -jnp.inf))
            ci.append(i1[i:i + 1] * N_KEYS + i2[0:SUBLANES])
        cs.append(s1[SUBLANES:k] + s2[0:1])
        ci.append(i1[SUBLANES:k] * N_KEYS + i2[0:1])
        cand_s = jnp.concatenate(cs, axis=0)
        cand_i = jnp.concatenate(ci, axis=0)
        top_s, pos = _topk_rows(cand_s, k)
        prow = lax.broadcasted_iota(I32, cand_s.shape, 0).astype(F32)
        sel = [jnp.sum(jnp.where(prow == pos[r:r + 1], cand_i, 0.0), axis=0, keepdims=True) for r in range(k)]
        top_i = jnp.concatenate(sel, axis=0)
        ex = jnp.exp(top_s - top_s[0:1])
        gate_rows.append(ex / jnp.sum(ex, axis=0, keepdims=True))
        idx_rows.append(top_i)
    idx_t = jnp.concatenate(idx_rows, axis=0)
    gate_t = jnp.concatenate([gate_rows[hh][r:r + 1] for r in range(k) for hh in range(n_heads)], axis=0)
    idx_ref[...] = (idx_t * 4.0).T.astype(I32)
    gate_ref[...] = gate_t.T


def _route_call(x1, mod8, g2, wq, keys, *, tb):
    b, s, d = x1.shape
    n_heads = keys.shape[0]
    nsel = n_heads * PEER_TOPK
    nblk = s // tb
    kern = functools.partial(_route_kernel, n_heads=n_heads)
    return pl.pallas_call(
        kern,
        grid=(b, nblk),
        in_specs=[pl.BlockSpec((None, tb, d), lambda i, j: (i, j, 0)),
                  pl.BlockSpec((None, 8, d), lambda i, j: (i, 0, 0)),
                  _const_spec(g2.shape), _const_spec(wq.shape), _const_spec(keys.shape)],
        out_specs=[pl.BlockSpec((tb * SUBLANES, LANES), lambda i, j: (i * nblk + j, 0)),
                   pl.BlockSpec((tb, nsel), lambda i, j: (i * nblk + j, 0)),
                   pl.BlockSpec((tb, nsel), lambda i, j: (i * nblk + j, 0))],
        out_shape=[jax.ShapeDtypeStruct((b * s * SUBLANES, LANES), F32),
                   jax.ShapeDtypeStruct((b * s, nsel), I32),
                   jax.ShapeDtypeStruct((b * s, nsel), F32)],
        compiler_params=pltpu.CompilerParams(dimension_semantics=("arbitrary", "arbitrary"),
                                             vmem_limit_bytes=VMEM_LIMIT_MIX),
        name="peer_route",
    )(x1, mod8, g2, wq, keys)


def _pack_kernel(x_ref, o_ref, stage):
    r = x_ref.shape[0]
    for c in range(SUBLANES):
        stage[pl.ds(c, r, stride=SUBLANES), :] = x_ref[:, c * LANES:(c + 1) * LANES]
    o_ref[...] = pltpu.bitcast(stage[...].astype(BF16), I32)


def _pack_table(tab, *, rows=512):
    n, d = tab.shape
    assert d == SUBLANES * LANES and n % rows == 0
    return pl.pallas_call(
        _pack_kernel,
        grid=(n // rows,),
        in_specs=[pl.BlockSpec((rows, d), lambda i: (i, 0))],
        out_specs=pl.BlockSpec((rows * 4, LANES), lambda i: (i, 0)),
        out_shape=jax.ShapeDtypeStruct((n * 4, LANES), I32),
        scratch_shapes=[pltpu.VMEM((rows * SUBLANES, LANES), F32)],
        name="peer_table_pack",
    )(tab)


def _gather_tile(tab_ref, ibuf, slot, positions):
    pieces = [tab_ref[pl.ds(pl.multiple_of(ibuf[slot, p], 4), 4), :] for p in positions]
    return pltpu.bitcast(jnp.concatenate(pieces, axis=0), BF16)


def _stage_ids(idx_ref, ibuf, sem, first_token, slot, group, nsel):
    n = group * nsel
    return pltpu.make_async_copy(idx_ref.at[pl.ds(first_token * nsel, n)], ibuf.at[slot], sem.at[slot])


def _staged_token_loop(idx_ref, ibuf, sem, tbp, nsel, group, open_trip):
    ngroups = tbp // group

    def stage(g, slot):
        return _stage_ids(idx_ref, ibuf, sem, g * group, slot, group, nsel)

    for s in range(IDX_SLOTS - 1):
        stage(s, s).start()

    def trip(i, carry):
        per_token = open_trip(i)
        for s in range(IDX_SLOTS):
            g = i * IDX_SLOTS + s
            stage(g, s).wait()
            stage(jnp.minimum(g + IDX_SLOTS - 1, ngroups - 1), (s + IDX_SLOTS - 1) % IDX_SLOTS).start()
            for u in range(group):
                per_token(s * group + u, s, u)
        return carry

    lax.fori_loop(0, ngroups // IDX_SLOTS, trip, 0)
    for s in range(IDX_SLOTS - 1):
        stage(ngroups - 1, s).wait()


def _window(ref, first, n):
    return ref.at[pl.ds(pl.multiple_of(first, SUBLANES), n)]


def _split_bf16(x, parts):
    out = []
    for _ in range(parts - 1):
        p = x.astype(BF16)
        out.append(p)
        x = x - p.astype(F32)
    out.append(x.astype(BF16))
    return out


def _sel_order(q):
    return (q % SUBLANES) * PEER_TOPK + q // SUBLANES


def _peer_u_kernel(idx_ref, h_ref, gate_ref, tab_ref, act_ref, ibuf, sem, f_s, *, tbp, nsel):
    trip = U_GROUP * IDX_SLOTS
    sub = lax.broadcasted_iota(I32, (SUBLANES, LANES), 0)
    lane = lax.broadcasted_iota(I32, (SUBLANES, LANES), 1)
    diag = sub == (lane & (SUBLANES - 1))
    lane0 = (lane & (SUBLANES - 1)) == 0
    m4 = (sub & 4) == 0
    m2 = (sub & 2) == 0
    m1 = (sub & 1) == 0

    def fold(a, b, dist, mask):
        return jnp.where(mask, a + pltpu.roll(a, SUBLANES - dist, 0), b + pltpu.roll(b, dist, 0))

    def column_sums(p):
        a0, a1 = fold(p[0], p[4], 4, m4), fold(p[2], p[6], 4, m4)
        a2, a3 = fold(p[1], p[5], 4, m4), fold(p[3], p[7], 4, m4)
        return fold(fold(a0, a1, 2, m2), fold(a2, a3, 2, m2), 1, m1)

    def partial_scores(h_win, tt, slot, u):
        hv = h_win[tt * SUBLANES:(tt + 1) * SUBLANES, :]
        h16 = jnp.concatenate(_split_bf16(hv, 2), axis=0)
        z = []
        for j in range(nsel // SEL_TILE):
            m = _gather_tile(tab_ref, ibuf, slot, [u * nsel + j * SEL_TILE + i for i in range(SEL_TILE)])
            o = lax.dot_general(h16, m, (((1,), (1,)), ((), ())), preferred_element_type=F32)
            o = o[0:SUBLANES] + o[SUBLANES:2 * SUBLANES]
            z += [jnp.where(diag, o[:, c * LANES:(c + 1) * LANES], 0.0) for c in range(SEL_TILE * SUBLANES // LANES)]
        f_s[tt] = column_sums(z)

    def finish_trip(act_win, gate_win):
        out = []
        for grp in range(trip // SUBLANES):
            g = []
            for j in range(SUBLANES):
                f = f_s[grp * SUBLANES + j]
                for sh in (4, 2, 1):
                    f = f + pltpu.roll(f, LANES - sh, 1)
                f = jnp.where(lane0, f, 0.0)
                g.append(pltpu.roll(f, 0, 1, stride=1, stride_axis=0))
            out.append(column_sums(g))
        act_win[...] = _gelu(jnp.concatenate(out, axis=0)) * gate_win[...]

    f_s[...] = jnp.zeros_like(f_s)

    def open_trip(i):
        tok0 = i * trip
        prev0 = jnp.maximum(tok0 - trip, 0)
        finish_trip(_window(act_ref, prev0, trip), _window(gate_ref, prev0, trip))
        h_win = _window(h_ref, tok0 * SUBLANES, trip * SUBLANES)
        return functools.partial(partial_scores, h_win)

    _staged_token_loop(idx_ref, ibuf, sem, tbp, nsel, U_GROUP, open_trip)
    finish_trip(_window(act_ref, tbp - trip, trip), _window(gate_ref, tbp - trip, trip))


def _peer_v_kernel(idx_ref, act_ref, tab_ref, rep_ref, y_ref, ibuf, sem, arep_s, *, tbp, nsel):
    trip = V_GROUP * IDX_SLOTS
    width = nsel * SUBLANES
    diag = (lax.broadcasted_iota(I32, (SUBLANES, width), 0)
            == (lax.broadcasted_iota(I32, (SUBLANES, width), 1) & (SUBLANES - 1)))

    def open_trip(i):
        tok0 = i * trip
        parts = jnp.concatenate(_split_bf16(act_ref[pl.ds(pl.multiple_of(tok0, trip), trip), :], ACT_PARTS), axis=0)
        arep_s[...] = jnp.dot(parts, rep_ref[...], preferred_element_type=F32)
        y_win = _window(y_ref, tok0 * SUBLANES, trip * SUBLANES)

        def per_token(tt, slot, u):
            lhs = jnp.concatenate(
                [jnp.where(diag, jnp.broadcast_to(arep_s[p * trip + tt:p * trip + tt + 1, :], (SUBLANES, width)), 0.0)
                 for p in range(ACT_PARTS)], axis=0).astype(BF16)
            m = _gather_tile(tab_ref, ibuf, slot, [u * nsel + _sel_order(q) for q in range(nsel)])
            acc = jnp.dot(lhs, m, preferred_element_type=F32)
            y = acc[0:SUBLANES]
            for p in range(1, ACT_PARTS):
                y = y + acc[p * SUBLANES:(p + 1) * SUBLANES]
            y_win[tt * SUBLANES:(tt + 1) * SUBLANES, :] = y

        return per_token

    _staged_token_loop(idx_ref, ibuf, sem, tbp, nsel, V_GROUP, open_trip)


def _table_spec(shape):
    return pl.BlockSpec(shape, lambda i: (0, 0), pipeline_mode=pl.Buffered(1))


def _stage_scratch(nsel, group):
    return [pltpu.SMEM((IDX_SLOTS, group * nsel), I32), pltpu.SemaphoreType.DMA((IDX_SLOTS,))]


def _peer_u_call(idx_flat, h8, gate, tab, *, tbp):
    t, nsel = gate.shape
    kern = functools.partial(_peer_u_kernel, tbp=tbp, nsel=nsel)
    return pl.pallas_call(
        kern,
        grid=(t // tbp,),
        in_specs=[pl.BlockSpec((tbp * nsel,), lambda i: (i,)),
                  pl.BlockSpec((tbp * SUBLANES, LANES), lambda i: (i, 0)),
                  pl.BlockSpec((tbp, nsel), lambda i: (i, 0)),
                  _table_spec(tab.shape)],
        out_specs=pl.BlockSpec((tbp, nsel), lambda i: (i, 0)),
        out_shape=jax.ShapeDtypeStruct((t, nsel), F32),
        scratch_shapes=_stage_scratch(nsel, U_GROUP) + [pltpu.VMEM((U_GROUP * IDX_SLOTS, SUBLANES, LANES), F32)],
        compiler_params=pltpu.CompilerParams(dimension_semantics=("arbitrary",),
                                             vmem_limit_bytes=VMEM_LIMIT_PEER),
        name="peer_expert_scores",
    )(idx_flat, h8, gate, tab)


def _peer_v_call(idx_flat, act, tab, *, tbp):
    t, nsel = act.shape
    kern = functools.partial(_peer_v_kernel, tbp=tbp, nsel=nsel)
    rep = jnp.repeat(jnp.eye(nsel, dtype=BF16), SUBLANES, axis=1)
    return pl.pallas_call(
        kern,
        grid=(t // tbp,),
        in_specs=[pl.BlockSpec((tbp * nsel,), lambda i: (i,)),
                  pl.BlockSpec((tbp, nsel), lambda i: (i, 0)),
                  _table_spec(tab.shape),
                  pl.BlockSpec(rep.shape, lambda i: (0, 0))],
        out_specs=pl.BlockSpec((tbp * SUBLANES, LANES), lambda i: (i, 0)),
        out_shape=jax.ShapeDtypeStruct((t * SUBLANES, LANES), F32),
        scratch_shapes=_stage_scratch(nsel, V_GROUP)
                       + [pltpu.VMEM((ACT_PARTS * V_GROUP * IDX_SLOTS, nsel * SUBLANES), F32)],
        compiler_params=pltpu.CompilerParams(dimension_semantics=("arbitrary",),
                                             vmem_limit_bytes=VMEM_LIMIT_PEER),
        name="peer_expert_mix",
    )(idx_flat, act, tab, rep)


def _resid_kernel(x_ref, y_ref, mod_ref, g_ref, o_ref, *, final):
    tb = x_ref.shape[0]
    y = jnp.concatenate([y_ref[pl.ds(c, tb, stride=SUBLANES), :] for c in range(SUBLANES)], axis=-1)
    z = x_ref[...] + mod_ref[5:6, :] * y
    o_ref[...] = _rms(z) * g_ref[...] if final else z


def _resid_call(x1, y, mod8, g, *, tb, final):
    b, s, d = x1.shape
    nblk = s // tb
    return pl.pallas_call(
        functools.partial(_resid_kernel, final=final),
        grid=(b, nblk),
        in_specs=[pl.BlockSpec((None, tb, d), lambda i, j: (i, j, 0)),
                  pl.BlockSpec((tb * SUBLANES, LANES), lambda i, j: (i * nblk + j, 0)),
                  pl.BlockSpec((None, 8, d), lambda i, j: (i, 0, 0)),
                  _const_spec(g.shape)],
        out_specs=pl.BlockSpec((None, tb, d), lambda i, j: (i, j, 0)),
        out_shape=jax.ShapeDtypeStruct((b, s, d), F32),
        name="peer_residual_norm",
    )(x1, y, mod8, g)


def _pick_block(n, pref):
    while n % pref:
        pref //= 2
    return pref


def kernel(x, c, ada_w, ada_b, norm1_g, w_in, lb_gamma, hgrn_norm_g, gmlp_ln_g, gmlp_ln_b, spatial_w,
           spatial_b, gmlp_norm_g, w_out, norm2_g, peer_wq, peer_keys, peer_u, peer_v, final_g):
    b, s, d = x.shape
    depth = ada_w.shape[0]
    t = b * s
    lower_bounds = jnp.cumsum(jax.nn.softmax(lb_gamma.astype(F32), axis=0), axis=0)
    c_pad = jnp.zeros((8, d), F32).at[:b].set(c)
    tb_mix = _pick_block(s, 256)
    tb_route = _pick_block(s, 256)
    tb_res = _pick_block(s, 512)
    tbp = _pick_block(t, 512)
    assert tbp % (max(U_GROUP, V_GROUP) * IDX_SLOTS) == 0
    for l in range(depth):
        mod = _mod_call(c_pad, ada_w[l], ada_b[l])[:b]
        mod8 = jnp.concatenate([mod.reshape(b, 6, d), jnp.zeros((b, 2, d), F32)], axis=1)
        x1 = _mix_call(x, mod8, norm1_g[l][None], w_in[l].astype(BF16), lower_bounds[l][None],
                       hgrn_norm_g[l][None], gmlp_ln_g[l][None], gmlp_ln_b[l][None], gmlp_norm_g[l][None],
                       spatial_w[l], spatial_b[l].T, w_out[l].astype(BF16), tb=tb_mix)
        h2, idx4, gate = _route_call(x1, mod8, norm2_g[l][None], peer_wq[l].astype(BF16), peer_keys[l],
                                     tb=tb_route)
        idx_flat = idx4.reshape(-1)
        act = _peer_u_call(idx_flat, h2, gate, _pack_table(peer_u[l]), tbp=tbp)
        y8 = _peer_v_call(idx_flat, act, _pack_table(peer_v[l]), tbp=tbp)
        last = l == depth - 1
        x = _resid_call(x1, y8, mod8, final_g[None] if last else jnp.ones((1, d), F32),
                        tb=tb_res, final=last)
    return x
```

```python
import functools

import jax
import jax.numpy as jnp
from jax import lax
from jax.experimental import pallas as pl
from jax.experimental.pallas import tpu as pltpu

F32 = jnp.float32
BF16 = jnp.bfloat16
I32 = jnp.int32
HIGHEST = lax.Precision.HIGHEST
NORM_EPS = 1e-6

LANES = 128
SUBLANES = 8
HEAD_DIM = 128
GMLP_CHUNK = 128
SUB = 16
PEER_TOPK = 16
N_KEYS = 128
IDX_SLOTS = 2
U_GROUP = 32
V_GROUP = 64
SIM_TILE = 256
SEL_TILE = 32
ACT_PARTS = 2
VMEM_LIMIT_MIX = 48 * 1024 * 1024
VMEM_LIMIT_PEER = 52 * 1024 * 1024


def _gelu(x):
    return 0.5 * x * (1.0 + jnp.tanh(0.7978845608028654 * (x + 0.044715 * (x * x * x))))


def _sigmoid(x):
    return 1.0 / (1.0 + jnp.exp(-x))


def _rms(x, eps=NORM_EPS):
    return x * lax.rsqrt(jnp.mean(x * x, axis=-1, keepdims=True) + eps)


def _const_spec(shape):
    nd = len(shape)
    return pl.BlockSpec(shape, lambda *_: (0,) * nd)


def _mod_kernel(c_ref, w_ref, b_ref, o_ref):
    c = c_ref[...]
    ca = c * _sigmoid(c)
    o_ref[...] = jnp.dot(ca, w_ref[...], precision=HIGHEST, preferred_element_type=F32) + b_ref[...]


def _mod_call(c_pad, w, b):
    rows, d = c_pad.shape
    n = w.shape[1]
    bn = 1536 if n % 1536 == 0 else n
    return pl.pallas_call(
        _mod_kernel,
        grid=(n // bn,),
        in_specs=[pl.BlockSpec((rows, d), lambda j: (0, 0)),
                  pl.BlockSpec((d, bn), lambda j: (0, j)),
                  pl.BlockSpec((1, bn), lambda j: (0, j))],
        out_specs=pl.BlockSpec((rows, bn), lambda j: (0, j)),
        out_shape=jax.ShapeDtypeStruct((rows, n), F32),
        name="adaln_mod",
    )(c_pad, w, b.reshape(1, n))


def _mix_kernel(x_ref, mod_ref, g1_ref, win_ref, lb_ref, hg_ref, lng_ref, lnb_ref, gng_ref,
                sw_ref, sbt_ref, wout_ref, o_ref,
                state, qe_s, ke_s, qf_s, kg_s, bb_s, vv_s, cat_s, *, tb, hw, gw):
    n_heads = hw // HEAD_DIM
    n_groups = gw // HEAD_DIM

    @pl.when(pl.program_id(1) == 0)
    def _():
        state[...] = jnp.zeros_like(state)

    x = x_ref[...]
    shift, scale, gate = mod_ref[0:1, :], mod_ref[1:2, :], mod_ref[2:3, :]
    h = (_rms(x) * g1_ref[...]) * (1.0 + scale) + shift
    proj = jnp.dot(h.astype(BF16), win_ref[...], preferred_element_type=F32)

    lb = lb_ref[...]
    q = proj[:, 0:hw]
    fz = proj[:, hw:2 * hw]
    e = jnp.exp(-jnp.abs(fz))
    r = 1.0 / (1.0 + e)
    pos = fz >= 0.0
    sig = jnp.where(pos, r, e * r)
    sig_neg = jnp.where(pos, e * r, r)
    logf = jnp.log(lb + (1.0 - lb) * sig)
    kg = (1.0 - lb) * sig_neg
    qf = q * _sigmoid(q)
    ri = lax.broadcasted_iota(I32, (tb, tb), 0)
    ci = lax.broadcasted_iota(I32, (tb, tb), 1)
    same = (ri // SUB) == (ci // SUB)
    tri_rows = jnp.where(same & (ci <= ri), 1.0, 0.0).astype(BF16)
    tri = jnp.concatenate([tri_rows, jnp.where(same & (ci > ri), 1.0, 0.0).astype(BF16)], axis=0)
    sums = sum(jnp.dot(tri, part, preferred_element_type=F32) for part in _split_bf16(logf, 3))
    bb = sums[0:tb]
    rem = sums[tb:2 * tb]
    qe_s[...] = qf * jnp.exp(bb)
    ke_s[...] = kg * jnp.exp(rem)
    qf_s[...] = qf
    kg_s[...] = kg
    bb_s[...] = bb
    vv_s[...] = proj[:, 2 * hw:3 * hw]

    rows16 = lax.broadcasted_iota(I32, (SUB, HEAD_DIM), 0)
    rows128 = lax.broadcasted_iota(I32, (GMLP_CHUNK, HEAD_DIM), 0)
    steps_per_chunk = GMLP_CHUNK // SUB

    def chunk_body(c, carry):
        c0 = pl.multiple_of(c * GMLP_CHUNK, GMLP_CHUNK)
        for hd in range(n_heads):
            cs = slice(hd * HEAD_DIM, (hd + 1) * HEAD_DIM)
            qe_c = qe_s[pl.ds(c0, GMLP_CHUNK), cs]
            ke_c = ke_s[pl.ds(c0, GMLP_CHUNK), cs]
            qf_c = qf_s[pl.ds(c0, GMLP_CHUNK), cs]
            kg_c = kg_s[pl.ds(c0, GMLP_CHUNK), cs]
            bb_c = bb_s[pl.ds(c0, GMLP_CHUNK), cs]
            v_c = vv_s[pl.ds(c0, GMLP_CHUNK), cs]
            v_t = v_c.T.astype(BF16)
            st = state[hd]
            outs = []
            for j in range(steps_per_chunk):
                rs = slice(j * SUB, (j + 1) * SUB)
                bb_b, qf_b, kg_b, v_b = bb_c[rs], qf_c[rs], kg_c[rs], v_c[rs]
                o = lax.dot_general(qe_c[rs].astype(BF16), st.astype(BF16), (((1,), (1,)), ((), ())),
                                    preferred_element_type=F32)
                for s in range(SUB):
                    dec = jnp.where(rows16 >= s, jnp.exp(bb_b - bb_b[s:s + 1, :]), 0.0)
                    w = jnp.sum(qf_b * dec * kg_b[s:s + 1, :], axis=-1, keepdims=True)
                    o = o + w * v_b[s:s + 1, :]
                outs.append(o)
                in_step = (rows128 >= j * SUB) & (rows128 < (j + 1) * SUB)
                ke_m = jnp.where(in_step, ke_c, 0.0).astype(BF16)
                upd = jnp.dot(v_t, ke_m, preferred_element_type=F32)
                st = jnp.exp(bb_b[SUB - 1:SUB, :]) * st + upd
            state[hd] = st
            o_c = jnp.concatenate(outs, axis=0)
            cat_s[pl.ds(c0, GMLP_CHUNK), cs] = _rms(o_c) * hg_ref[:, cs]
        return carry

    lax.fori_loop(0, tb // GMLP_CHUNK, chunk_body, 0)

    og = proj[:, 3 * hw:4 * hw]
    o_all = cat_s[:, 0:hw] * (og * _sigmoid(og))

    u = _gelu(proj[:, 4 * hw:4 * hw + gw])
    v = _gelu(proj[:, 4 * hw + gw:4 * hw + 2 * gw])
    mu = jnp.mean(v, axis=-1, keepdims=True)
    vc = v - mu
    var = jnp.mean(vc * vc, axis=-1, keepdims=True)
    vln = (vc * lax.rsqrt(var + NORM_EPS)) * lng_ref[...] + lnb_ref[...]
    tri = (lax.broadcasted_iota(I32, (GMLP_CHUNK, GMLP_CHUNK), 1)
           <= lax.broadcasted_iota(I32, (GMLP_CHUNK, GMLP_CHUNK), 0))
    gm_groups = []
    for g in range(n_groups):
        gs = slice(g * HEAD_DIM, (g + 1) * HEAD_DIM)
        wm = jnp.where(tri, sw_ref[g], 0.0).astype(BF16)
        bias = sbt_ref[:, g:g + 1]
        parts = []
        for cidx in range(tb // GMLP_CHUNK):
            rs = slice(cidx * GMLP_CHUNK, (cidx + 1) * GMLP_CHUNK)
            mixed = jnp.dot(wm, vln[rs, gs].astype(BF16), preferred_element_type=F32) + bias
            parts.append(u[rs, gs] * mixed)
        gmg = jnp.concatenate(parts, axis=0) if len(parts) > 1 else parts[0]
        gm_groups.append(_rms(gmg) * gng_ref[:, gs])
    gm = jnp.concatenate(gm_groups, axis=-1)

    cat = jnp.concatenate([o_all, gm], axis=-1).astype(BF16)
    mixed_out = jnp.dot(cat, wout_ref[...], preferred_element_type=F32)
    o_ref[...] = x + gate * mixed_out


def _mix_call(x, mod8, g1, w_in, lb, hg, lng, lnb, gng, sw, sbt, w_out, *, tb):
    b, s, d = x.shape
    hw = lb.shape[1]
    gw = lng.shape[1]
    n_heads = hw // HEAD_DIM
    kern = functools.partial(_mix_kernel, tb=tb, hw=hw, gw=gw)
    return pl.pallas_call(
        kern,
        grid=(b, s // tb),
        in_specs=[pl.BlockSpec((None, tb, d), lambda i, j: (i, j, 0)),
                  pl.BlockSpec((None, 8, d), lambda i, j: (i, 0, 0)),
                  _const_spec(g1.shape), _const_spec(w_in.shape), _const_spec(lb.shape),
                  _const_spec(hg.shape), _const_spec(lng.shape), _const_spec(lnb.shape),
                  _const_spec(gng.shape), _const_spec(sw.shape), _const_spec(sbt.shape),
                  _const_spec(w_out.shape)],
        out_specs=pl.BlockSpec((None, tb, d), lambda i, j: (i, j, 0)),
        out_shape=jax.ShapeDtypeStruct((b, s, d), F32),
        scratch_shapes=[pltpu.VMEM((n_heads, HEAD_DIM, HEAD_DIM), F32)]
                       + [pltpu.VMEM((tb, hw), F32) for _ in range(7)],
        compiler_params=pltpu.CompilerParams(dimension_semantics=("arbitrary", "arbitrary"),
                                             vmem_limit_bytes=VMEM_LIMIT_MIX),
        name="hgrn_gmlp_mixer",
    )(x, mod8, g1, w_in, lb, hg, lng, lnb, gng, sw, sbt, w_out)


def _topk_rows(x, k):
    n = x.shape[0]
    rid = lax.broadcasted_iota(I32, x.shape, 0).astype(F32)
    vals, ids = [], []
    for _ in range(k):
        m = jnp.max(x, axis=0, keepdims=True)
        am = jnp.min(jnp.where(x == m, rid, float(n)), axis=0, keepdims=True)
        vals.append(m)
        ids.append(am)
        x = jnp.where(rid == am, -jnp.inf, x)
    return jnp.concatenate(vals, axis=0), jnp.concatenate(ids, axis=0)


def _head_topk(sim1, sim2):
    k = PEER_TOPK
    t = sim1.shape[1]
    jrow8 = lax.broadcasted_iota(I32, (SUBLANES, t), 0)
    (s1, i1), (s2, i2) = _topk_rows(sim1, k), _topk_rows(sim2, k)
    cs = [s1[0:1] + s2]
    ci = [i1[0:1] * N_KEYS + i2]
    for i in range(1, SUBLANES):
        nvalid = k // (i + 1)
        cs.append(jnp.where(jrow8 < nvalid, s1[i:i + 1] + s2[0:SUBLANES], -jnp.inf))
        ci.append(i1[i:i + 1] * N_KEYS + i2[0:SUBLANES])
    cs.append(s1[SUBLANES:k] + s2[0:1])
    ci.append(i1[SUBLANES:k] * N_KEYS + i2[0:1])
    cand_s = jnp.concatenate(cs, axis=0)
    cand_i = jnp.concatenate(ci, axis=0)
    top_s, pos = _topk_rows(cand_s, k)
    prow = lax.broadcasted_iota(I32, cand_s.shape, 0).astype(F32)
    sel = [jnp.sum(jnp.where(prow == pos[r:r + 1], cand_i, 0.0), axis=0, keepdims=True) for r in range(k)]
    ex = jnp.exp(top_s - top_s[0:1])
    return jnp.concatenate(sel, axis=0), ex / jnp.sum(ex, axis=0, keepdims=True)


def _sim_kernel(x_ref, mod_ref, g2_ref, wq_ref, keys_ref, h_ref, sim_ref, *, n_heads):
    x = x_ref[...]
    shift, scale = mod_ref[3:4, :], mod_ref[4:5, :]
    h = (_rms(x) * g2_ref[...]) * (1.0 + scale) + shift
    tb = x.shape[0]
    for c in range(SUBLANES):
        h_ref[pl.ds(c, tb, stride=SUBLANES), :] = h[:, c * LANES:(c + 1) * LANES]
    q = jnp.dot(h.astype(BF16), wq_ref[...], preferred_element_type=F32)
    for hp in range(2 * n_heads):
        sim_ref[hp] = lax.dot_general(keys_ref[hp // 2, hp % 2].astype(BF16),
                                      q[:, hp * HEAD_DIM:(hp + 1) * HEAD_DIM].astype(BF16),
                                      (((1,), (1,)), ((), ())), preferred_element_type=F32)


def _sim_call(x1, mod8, g2, wq, keys):
    b, s, d = x1.shape
    n_heads = keys.shape[0]
    tb = SIM_TILE
    nblk = s // tb
    kern = functools.partial(_sim_kernel, n_heads=n_heads)
    return pl.pallas_call(
        kern,
        grid=(b, nblk),
        in_specs=[pl.BlockSpec((None, tb, d), lambda i, j: (i, j, 0)),
                  pl.BlockSpec((None, 8, d), lambda i, j: (i, 0, 0)),
                  _const_spec(g2.shape), _const_spec(wq.shape), _const_spec(keys.shape)],
        out_specs=[pl.BlockSpec((tb * SUBLANES, LANES), lambda i, j: (i * nblk + j, 0)),
                   pl.BlockSpec((None, 2 * n_heads, N_KEYS, tb), lambda i, j: (i * nblk + j, 0, 0, 0))],
        out_shape=[jax.ShapeDtypeStruct((b * s * SUBLANES, LANES), F32),
                   jax.ShapeDtypeStruct((b * nblk, 2 * n_heads, N_KEYS, tb), F32)],
        compiler_params=pltpu.CompilerParams(dimension_semantics=("arbitrary", "arbitrary"),
                                             vmem_limit_bytes=VMEM_LIMIT_MIX),
        name="peer_similarities",
    )(x1, mod8, g2, wq, keys)


def _pack_kernel(x_ref, o_ref, stage):
    r = x_ref.shape[0]
    for c in range(SUBLANES):
        stage[pl.ds(c, r, stride=SUBLANES), :] = x_ref[:, c * LANES:(c + 1) * LANES]
    o_ref[...] = pltpu.bitcast(stage[...].astype(BF16), I32)


def _pack_table(tab, *, rows=512):
    n, d = tab.shape
    assert d == SUBLANES * LANES and n % rows == 0
    return pl.pallas_call(
        _pack_kernel,
        grid=(n // rows,),
        in_specs=[pl.BlockSpec((rows, d), lambda i: (i, 0))],
        out_specs=pl.BlockSpec((rows * 4, LANES), lambda i: (i, 0)),
        out_shape=jax.ShapeDtypeStruct((n * 4, LANES), I32),
        scratch_shapes=[pltpu.VMEM((rows * SUBLANES, LANES), F32)],
        name="peer_table_pack",
    )(tab)


def _gather_tile(tab_ref, ibuf, slot, positions):
    pieces = [tab_ref[pl.ds(pl.multiple_of(ibuf[slot, u, k], 4), 4), :] for u, k in positions]
    return pltpu.bitcast(jnp.concatenate(pieces, axis=0), BF16)


def _stage_ids(idx_ref, ibuf, sem, first_token, slot, group):
    return pltpu.make_async_copy(idx_ref.at[pl.ds(first_token, group)], ibuf.at[slot], sem.at[slot])


def _staged_token_loop(idx_ref, ibuf, sem, tbp, nsel, group, open_trip):
    ngroups = tbp // group

    def stage(g, slot):
        return _stage_ids(idx_ref, ibuf, sem, pl.multiple_of(g * group, group), slot, group)

    for s in range(IDX_SLOTS - 1):
        stage(s, s).start()

    def trip(i, carry):
        per_token, per_group = open_trip(i)
        for s in range(IDX_SLOTS):
            g = i * IDX_SLOTS + s
            stage(g, s).wait()
            stage(jnp.minimum(g + IDX_SLOTS - 1, ngroups - 1), (s + IDX_SLOTS - 1) % IDX_SLOTS).start()
            if per_group is not None:
                per_group(s)
            for u in range(group):
                per_token(s * group + u, s, u)
        return carry

    lax.fori_loop(0, ngroups // IDX_SLOTS, trip, 0)
    for s in range(IDX_SLOTS - 1):
        stage(ngroups - 1, s).wait()


def _window(ref, first, n):
    return ref.at[pl.ds(pl.multiple_of(first, SUBLANES), n)]


def _split_bf16(x, parts):
    out = []
    for _ in range(parts - 1):
        p = x.astype(BF16)
        out.append(p)
        x = x - p.astype(F32)
    out.append(x.astype(BF16))
    return out


def _sel_order(q):
    return (q % SUBLANES) * PEER_TOPK + q // SUBLANES


def _peer_u_kernel(sim_ref, h_ref, tab_ref, bsel_ref, act_ref, idx_ref, ibuf, sem, f_s, idt_s, gt_s, ids_s, gate_s,
                   *, tbp, nsel):
    trip = U_GROUP * IDX_SLOTS
    n_heads = nsel // PEER_TOPK
    assert (tbp // SIM_TILE) * n_heads == tbp // U_GROUP

    @pl.when(pl.program_id(0) == 0)
    def _():
        ids_s[...] = jnp.zeros_like(ids_s)
        gate_s[...] = jnp.zeros_like(gate_s)

    idx_ref[...] = ids_s[...]

    def route_unit(w):
        tile, hh = w // n_heads, w % n_heads
        ids, gates = _head_topk(sim_ref[tile, 2 * hh], sim_ref[tile, 2 * hh + 1])
        lanes = pl.ds(pl.multiple_of(tile * SIM_TILE, SIM_TILE), SIM_TILE)
        idt_s[pl.ds(pl.multiple_of(hh * PEER_TOPK, PEER_TOPK), PEER_TOPK), lanes] = ids * 4.0
        for r in range(PEER_TOPK):
            gt_s[pl.ds(r * n_heads + hh, 1), lanes] = gates[r:r + 1]

    sub = lax.broadcasted_iota(I32, (SUBLANES, LANES), 0)
    lane = lax.broadcasted_iota(I32, (SUBLANES, LANES), 1)
    diag = sub == (lane & (SUBLANES - 1))
    m4 = (sub & 4) == 0
    m2 = (sub & 2) == 0
    m1 = (sub & 1) == 0

    def fold(a, b, dist, mask):
        return jnp.where(mask, a + pltpu.roll(a, SUBLANES - dist, 0), b + pltpu.roll(b, dist, 0))

    def column_sums(p):
        a0, a1 = fold(p[0], p[4], 4, m4), fold(p[2], p[6], 4, m4)
        a2, a3 = fold(p[1], p[5], 4, m4), fold(p[3], p[7], 4, m4)
        return fold(fold(a0, a1, 2, m2), fold(a2, a3, 2, m2), 1, m1)

    def partial_scores(h_win, tt, slot, u):
        hv = h_win[tt * SUBLANES:(tt + 1) * SUBLANES, :]
        h16 = jnp.concatenate(_split_bf16(hv, 2), axis=0)
        z = []
        for j in range(nsel // SEL_TILE):
            m = _gather_tile(tab_ref, ibuf, slot, [(u, j * SEL_TILE + i) for i in range(SEL_TILE)])
            o = lax.dot_general(h16, m, (((1,), (1,)), ((), ())), preferred_element_type=F32)
            o = o[0:SUBLANES] + o[SUBLANES:2 * SUBLANES]
            z += [jnp.where(diag, o[:, c * LANES:(c + 1) * LANES], 0.0) for c in range(SEL_TILE * SUBLANES // LANES)]
        f_s[tt * SUBLANES:(tt + 1) * SUBLANES, :] = column_sums(z)

    def finish_trip(act_win, gate_win):
        acc = None
        for v in range(SUBLANES):
            xv = f_s[pl.ds(v, trip, stride=SUBLANES), :]
            d = jnp.dot(jnp.concatenate(_split_bf16(xv, 3), axis=0), bsel_ref[v], preferred_element_type=F32)
            acc = d if acc is None else acc + d
        s = acc[0:trip] + acc[trip:2 * trip] + acc[2 * trip:3 * trip]
        act_win[...] = _gelu(s) * gate_win[...]

    f_s[...] = jnp.zeros_like(f_s)

    def open_trip(i):
        tok0 = i * trip
        prev0 = jnp.maximum(tok0 - trip, 0)
        finish_trip(_window(act_ref, prev0, trip), _window(gate_s, prev0, trip))
        h_win = _window(h_ref, tok0 * SUBLANES, trip * SUBLANES)
        return functools.partial(partial_scores, h_win), lambda s: route_unit(i * IDX_SLOTS + s)

    _staged_token_loop(ids_s, ibuf, sem, tbp, nsel, U_GROUP, open_trip)
    finish_trip(_window(act_ref, tbp - trip, trip), _window(gate_s, tbp - trip, trip))
    ids_s[...] = idt_s[...].T.astype(I32)
    gate_s[...] = gt_s[...].T


def _peer_v_kernel(idx_ref, act_ref, tab_ref, rep_ref, y_ref, ibuf, sem, arep_s, *, tbp, nsel):
    trip = V_GROUP * IDX_SLOTS
    width = nsel * SUBLANES
    diag = (lax.broadcasted_iota(I32, (SUBLANES, width), 0)
            == (lax.broadcasted_iota(I32, (SUBLANES, width), 1) & (SUBLANES - 1)))

    def open_trip(i):
        tok0 = i * trip
        parts = jnp.concatenate(_split_bf16(act_ref[pl.ds(pl.multiple_of(tok0, trip), trip), :], ACT_PARTS), axis=0)
        arep_s[...] = jnp.dot(parts, rep_ref[...], preferred_element_type=F32)
        y_win = _window(y_ref, tok0 * SUBLANES, trip * SUBLANES)

        def per_token(tt, slot, u):
            lhs = jnp.concatenate(
                [jnp.where(diag, jnp.broadcast_to(arep_s[p * trip + tt:p * trip + tt + 1, :], (SUBLANES, width)), 0.0)
                 for p in range(ACT_PARTS)], axis=0).astype(BF16)
            m = _gather_tile(tab_ref, ibuf, slot, [(u, _sel_order(q)) for q in range(nsel)])
            acc = jnp.dot(lhs, m, preferred_element_type=F32)
            y = acc[0:SUBLANES]
            for p in range(1, ACT_PARTS):
                y = y + acc[p * SUBLANES:(p + 1) * SUBLANES]
            y_win[tt * SUBLANES:(tt + 1) * SUBLANES, :] = y

        return per_token, None

    _staged_token_loop(idx_ref, ibuf, sem, tbp, nsel, V_GROUP, open_trip)


def _table_spec(shape):
    return pl.BlockSpec(shape, lambda i: (0, 0), pipeline_mode=pl.Buffered(1))


def _stage_scratch(nsel, group):
    return [pltpu.SMEM((IDX_SLOTS, group, nsel), I32), pltpu.SemaphoreType.DMA((IDX_SLOTS,))]


def _peer_u_call(sims, h8, tab, *, tbp):
    t = h8.shape[0] // SUBLANES
    nsel = (sims.shape[1] // 2) * PEER_TOPK
    nblk = t // tbp
    tiles = tbp // SIM_TILE
    assert sims.shape[3] == SIM_TILE and tbp % SIM_TILE == 0

    def routed(i):
        return jnp.minimum(i, nblk - 1)

    def scored(i):
        return jnp.maximum(i - 1, 0)

    kern = functools.partial(_peer_u_kernel, tbp=tbp, nsel=nsel)
    lane = jnp.arange(LANES)
    bsel = (lane[None, None, :] == (SUBLANES * (lane // SUBLANES))[None, :, None]
            + jnp.arange(SUBLANES)[:, None, None]).astype(BF16)
    return pl.pallas_call(
        kern,
        grid=(nblk + 1,),
        in_specs=[pl.BlockSpec((tiles,) + sims.shape[1:], lambda i: (routed(i), 0, 0, 0)),
                  pl.BlockSpec((tbp * SUBLANES, LANES), lambda i: (scored(i), 0)),
                  _table_spec(tab.shape),
                  pl.BlockSpec(bsel.shape, lambda i: (0, 0, 0))],
        out_specs=[pl.BlockSpec((tbp, nsel), lambda i: (scored(i), 0)),
                   pl.BlockSpec((tbp, nsel), lambda i: (scored(i), 0))],
        out_shape=[jax.ShapeDtypeStruct((t, nsel), F32), jax.ShapeDtypeStruct((t, nsel), I32)],
        scratch_shapes=_stage_scratch(nsel, U_GROUP) + [
            pltpu.VMEM((U_GROUP * IDX_SLOTS * SUBLANES, LANES), F32),
            pltpu.VMEM((nsel, tbp), F32), pltpu.VMEM((nsel, tbp), F32),
            pltpu.VMEM((tbp, nsel), I32), pltpu.VMEM((tbp, nsel), F32)],
        compiler_params=pltpu.CompilerParams(dimension_semantics=("arbitrary",),
                                             vmem_limit_bytes=VMEM_LIMIT_PEER),
        name="peer_route_scores",
    )(sims, h8, tab, bsel)


def _peer_v_call(idx4, act, tab, *, tbp):
    t, nsel = act.shape
    kern = functools.partial(_peer_v_kernel, tbp=tbp, nsel=nsel)
    rep = jnp.repeat(jnp.eye(nsel, dtype=BF16), SUBLANES, axis=1)
    return pl.pallas_call(
        kern,
        grid=(t // tbp,),
        in_specs=[pl.BlockSpec((tbp, nsel), lambda i: (i, 0)),
                  pl.BlockSpec((tbp, nsel), lambda i: (i, 0)),
                  _table_spec(tab.shape),
                  pl.BlockSpec(rep.shape, lambda i: (0, 0))],
        out_specs=pl.BlockSpec((tbp * SUBLANES, LANES), lambda i: (i, 0)),
        out_shape=jax.ShapeDtypeStruct((t * SUBLANES, LANES), F32),
        scratch_shapes=_stage_scratch(nsel, V_GROUP)
                       + [pltpu.VMEM((ACT_PARTS * V_GROUP * IDX_SLOTS, nsel * SUBLANES), F32)],
        compiler_params=pltpu.CompilerParams(dimension_semantics=("arbitrary",),
                                             vmem_limit_bytes=VMEM_LIMIT_PEER),
        name="peer_expert_mix",
    )(idx4, act, tab, rep)


def _resid_kernel(x_ref, y_ref, mod_ref, g_ref, o_ref, *, final):
    tb = x_ref.shape[0]
    y = jnp.concatenate([y_ref[pl.ds(c, tb, stride=SUBLANES), :] for c in range(SUBLANES)], axis=-1)
    z = x_ref[...] + mod_ref[5:6, :] * y
    o_ref[...] = _rms(z) * g_ref[...] if final else z


def _resid_call(x1, y, mod8, g, *, tb, final):
    b, s, d = x1.shape
    nblk = s // tb
    return pl.pallas_call(
        functools.partial(_resid_kernel, final=final),
        grid=(b, nblk),
        in_specs=[pl.BlockSpec((None, tb, d), lambda i, j: (i, j, 0)),
                  pl.BlockSpec((tb * SUBLANES, LANES), lambda i, j: (i * nblk + j, 0)),
                  pl.BlockSpec((None, 8, d), lambda i, j: (i, 0, 0)),
                  _const_spec(g.shape)],
        out_specs=pl.BlockSpec((None, tb, d), lambda i, j: (i, j, 0)),
        out_shape=jax.ShapeDtypeStruct((b, s, d), F32),
        name="peer_residual_norm",
    )(x1, y, mod8, g)


def _pick_block(n, pref):
    while n % pref:
        pref //= 2
    return pref


def kernel(x, c, ada_w, ada_b, norm1_g, w_in, lb_gamma, hgrn_norm_g, gmlp_ln_g, gmlp_ln_b, spatial_w,
           spatial_b, gmlp_norm_g, w_out, norm2_g, peer_wq, peer_keys, peer_u, peer_v, final_g):
    b, s, d = x.shape
    depth = ada_w.shape[0]
    t = b * s
    lower_bounds = jnp.cumsum(jax.nn.softmax(lb_gamma.astype(F32), axis=0), axis=0)
    c_pad = jnp.zeros((8, d), F32).at[:b].set(c)
    tb_mix = _pick_block(s, 256)
    tb_res = _pick_block(s, 512)
    tbp = _pick_block(t, 512)
    assert tbp % (max(U_GROUP, V_GROUP) * IDX_SLOTS) == 0
    for l in range(depth):
        mod = _mod_call(c_pad, ada_w[l], ada_b[l])[:b]
        mod8 = jnp.concatenate([mod.reshape(b, 6, d), jnp.zeros((b, 2, d), F32)], axis=1)
        x1 = _mix_call(x, mod8, norm1_g[l][None], w_in[l].astype(BF16), lower_bounds[l][None],
                       hgrn_norm_g[l][None], gmlp_ln_g[l][None], gmlp_ln_b[l][None], gmlp_norm_g[l][None],
                       spatial_w[l], spatial_b[l].T, w_out[l].astype(BF16), tb=tb_mix)
        h2, sims = _sim_call(x1, mod8, norm2_g[l][None], peer_wq[l].astype(BF16), peer_keys[l])
        act, idx4 = _peer_u_call(sims, h2, _pack_table(peer_u[l]), tbp=tbp)
        y8 = _peer_v_call(idx4, act, _pack_table(peer_v[l]), tbp=tbp)
        last = l == depth - 1
        x = _resid_call(x1, y8, mod8, final_g[None] if last else jnp.ones((1, d), F32),
                        tb=tb_res, final=last)
    return x
```

```python
import functools

import jax
import jax.numpy as jnp
from jax import lax
from jax.experimental import pallas as pl
from jax.experimental.pallas import tpu as pltpu

F32 = jnp.float32
BF16 = jnp.bfloat16
I32 = jnp.int32
HIGHEST = lax.Precision.HIGHEST
NORM_EPS = 1e-6

LANES = 128
SUBLANES = 8
HEAD_DIM = 128
GMLP_CHUNK = 128
SUB = 16
PEER_TOPK = 16
N_KEYS = 128
IDX_SLOTS = 2
U_GROUP = 32
V_GROUP = 64
SIM_TILE = 256
SEL_TILE = 32
ACT_PARTS = 2
VMEM_LIMIT_MIX = 48 * 1024 * 1024
VMEM_LIMIT_PEER = 52 * 1024 * 1024


def _gelu(x):
    return 0.5 * x * (1.0 + jnp.tanh(0.7978845608028654 * (x + 0.044715 * (x * x * x))))


def _sigmoid(x):
    return 1.0 / (1.0 + jnp.exp(-x))


def _rms(x, eps=NORM_EPS):
    return x * lax.rsqrt(jnp.mean(x * x, axis=-1, keepdims=True) + eps)


def _const_spec(shape):
    nd = len(shape)
    return pl.BlockSpec(shape, lambda *_: (0,) * nd)


def _mod_kernel(c_ref, w_ref, b_ref, o_ref):
    c = c_ref[...]
    ca = c * _sigmoid(c)
    o_ref[...] = jnp.dot(ca, w_ref[...], precision=HIGHEST, preferred_element_type=F32) + b_ref[...]


def _mod_call(c_pad, w, b):
    rows, d = c_pad.shape
    n = w.shape[1]
    bn = 1536 if n % 1536 == 0 else n
    return pl.pallas_call(
        _mod_kernel,
        grid=(n // bn,),
        in_specs=[pl.BlockSpec((rows, d), lambda j: (0, 0)),
                  pl.BlockSpec((d, bn), lambda j: (0, j)),
                  pl.BlockSpec((1, bn), lambda j: (0, j))],
        out_specs=pl.BlockSpec((rows, bn), lambda j: (0, j)),
        out_shape=jax.ShapeDtypeStruct((rows, n), F32),
        name="adaln_mod",
    )(c_pad, w, b.reshape(1, n))


def _mix_kernel(x_ref, mod_ref, g1_ref, win_ref, lb_ref, hg_ref, lng_ref, lnb_ref, gng_ref,
                sw_ref, sbt_ref, wout_ref, o_ref,
                state, qe_s, ke_s, qf_s, kg_s, bb_s, vv_s, cat_s, *, tb, hw, gw):
    n_heads = hw // HEAD_DIM
    n_groups = gw // HEAD_DIM

    @pl.when(pl.program_id(1) == 0)
    def _():
        state[...] = jnp.zeros_like(state)

    x = x_ref[...]
    shift, scale, gate = mod_ref[0:1, :], mod_ref[1:2, :], mod_ref[2:3, :]
    h = (_rms(x) * g1_ref[...]) * (1.0 + scale) + shift
    proj = jnp.dot(h.astype(BF16), win_ref[...], preferred_element_type=F32)

    lb = lb_ref[...]
    q = proj[:, 0:hw]
    fz = proj[:, hw:2 * hw]
    e = jnp.exp(-jnp.abs(fz))
    r = 1.0 / (1.0 + e)
    pos = fz >= 0.0
    sig = jnp.where(pos, r, e * r)
    sig_neg = jnp.where(pos, e * r, r)
    logf = jnp.log(lb + (1.0 - lb) * sig)
    kg = (1.0 - lb) * sig_neg
    qf = q * _sigmoid(q)
    ri = lax.broadcasted_iota(I32, (tb, tb), 0)
    ci = lax.broadcasted_iota(I32, (tb, tb), 1)
    same = (ri // SUB) == (ci // SUB)
    tri_rows = jnp.where(same & (ci <= ri), 1.0, 0.0).astype(BF16)
    tri = jnp.concatenate([tri_rows, jnp.where(same & (ci > ri), 1.0, 0.0).astype(BF16)], axis=0)
    sums = sum(jnp.dot(tri, part, preferred_element_type=F32) for part in _split_bf16(logf, 3))
    bb = sums[0:tb]
    rem = sums[tb:2 * tb]
    qe_s[...] = qf * jnp.exp(bb)
    ke_s[...] = kg * jnp.exp(rem)
    qf_s[...] = qf
    kg_s[...] = kg
    bb_s[...] = bb
    vv_s[...] = proj[:, 2 * hw:3 * hw]

    rows16 = lax.broadcasted_iota(I32, (SUB, HEAD_DIM), 0)
    rows128 = lax.broadcasted_iota(I32, (GMLP_CHUNK, HEAD_DIM), 0)
    steps_per_chunk = GMLP_CHUNK // SUB

    def chunk_body(c, carry):
        c0 = pl.multiple_of(c * GMLP_CHUNK, GMLP_CHUNK)
        for hd in range(n_heads):
            cs = slice(hd * HEAD_DIM, (hd + 1) * HEAD_DIM)
            ke_c = ke_s[pl.ds(c0, GMLP_CHUNK), cs]
            v_t = vv_s[pl.ds(c0, GMLP_CHUNK), cs].T.astype(BF16)
            st = state[hd]
            outs = []
            for j in range(steps_per_chunk):
                rs = pl.ds(c0 + j * SUB, SUB)
                bb_b, qf_b, kg_b, v_b = bb_s[rs, cs], qf_s[rs, cs], kg_s[rs, cs], vv_s[rs, cs]
                o = lax.dot_general(qe_s[rs, cs].astype(BF16), st.astype(BF16), (((1,), (1,)), ((), ())),
                                    preferred_element_type=F32)
                for s in range(SUB):
                    dec = jnp.where(rows16 >= s, jnp.exp(bb_b - bb_b[s:s + 1, :]), 0.0)
                    w = jnp.sum(qf_b * dec * kg_b[s:s + 1, :], axis=-1, keepdims=True)
                    o = o + w * v_b[s:s + 1, :]
                outs.append(o)
                in_step = (rows128 >= j * SUB) & (rows128 < (j + 1) * SUB)
                ke_m = jnp.where(in_step, ke_c, 0.0).astype(BF16)
                upd = jnp.dot(v_t, ke_m, preferred_element_type=F32)
                st = jnp.exp(bb_b[SUB - 1:SUB, :]) * st + upd
            state[hd] = st
            o_c = jnp.concatenate(outs, axis=0)
            cat_s[pl.ds(c0, GMLP_CHUNK), cs] = _rms(o_c) * hg_ref[:, cs]
        return carry

    lax.fori_loop(0, tb // GMLP_CHUNK, chunk_body, 0)

    og = proj[:, 3 * hw:4 * hw]
    o_all = cat_s[:, 0:hw] * (og * _sigmoid(og))

    u = _gelu(proj[:, 4 * hw:4 * hw + gw])
    v = _gelu(proj[:, 4 * hw + gw:4 * hw + 2 * gw])
    mu = jnp.mean(v, axis=-1, keepdims=True)
    vc = v - mu
    var = jnp.mean(vc * vc, axis=-1, keepdims=True)
    vln = (vc * lax.rsqrt(var + NORM_EPS)) * lng_ref[...] + lnb_ref[...]
    tri = (lax.broadcasted_iota(I32, (GMLP_CHUNK, GMLP_CHUNK), 1)
           <= lax.broadcasted_iota(I32, (GMLP_CHUNK, GMLP_CHUNK), 0))
    gm_groups = []
    for g in range(n_groups):
        gs = slice(g * HEAD_DIM, (g + 1) * HEAD_DIM)
        wm = jnp.where(tri, sw_ref[g], 0.0).astype(BF16)
        bias = sbt_ref[:, g:g + 1]
        parts = []
        for cidx in range(tb // GMLP_CHUNK):
            rs = slice(cidx * GMLP_CHUNK, (cidx + 1) * GMLP_CHUNK)
            mixed = jnp.dot(wm, vln[rs, gs].astype(BF16), preferred_element_type=F32) + bias
            parts.append(u[rs, gs] * mixed)
        gmg = jnp.concatenate(parts, axis=0) if len(parts) > 1 else parts[0]
        gm_groups.append(_rms(gmg) * gng_ref[:, gs])
    gm = jnp.concatenate(gm_groups, axis=-1)

    cat = jnp.concatenate([o_all, gm], axis=-1).astype(BF16)
    mixed_out = jnp.dot(cat, wout_ref[...], preferred_element_type=F32)
    o_ref[...] = x + gate * mixed_out


def _mix_call(x, mod8, g1, w_in, lb, hg, lng, lnb, gng, sw, sbt, w_out, *, tb):
    b, s, d = x.shape
    hw = lb.shape[1]
    gw = lng.shape[1]
    n_heads = hw // HEAD_DIM
    kern = functools.partial(_mix_kernel, tb=tb, hw=hw, gw=gw)
    return pl.pallas_call(
        kern,
        grid=(b, s // tb),
        in_specs=[pl.BlockSpec((None, tb, d), lambda i, j: (i, j, 0)),
                  pl.BlockSpec((None, 8, d), lambda i, j: (i, 0, 0)),
                  _const_spec(g1.shape), _const_spec(w_in.shape), _const_spec(lb.shape),
                  _const_spec(hg.shape), _const_spec(lng.shape), _const_spec(lnb.shape),
                  _const_spec(gng.shape), _const_spec(sw.shape), _const_spec(sbt.shape),
                  _const_spec(w_out.shape)],
        out_specs=pl.BlockSpec((None, tb, d), lambda i, j: (i, j, 0)),
        out_shape=jax.ShapeDtypeStruct((b, s, d), F32),
        scratch_shapes=[pltpu.VMEM((n_heads, HEAD_DIM, HEAD_DIM), F32)]
                       + [pltpu.VMEM((tb, hw), F32) for _ in range(7)],
        compiler_params=pltpu.CompilerParams(dimension_semantics=("arbitrary", "arbitrary"),
                                             vmem_limit_bytes=VMEM_LIMIT_MIX),
        name="hgrn_gmlp_mixer",
    )(x, mod8, g1, w_in, lb, hg, lng, lnb, gng, sw, sbt, w_out)


def _topk_rows(x, k):
    n = x.shape[0]
    rid = lax.broadcasted_iota(I32, x.shape, 0).astype(F32)
    vals, ids = [], []
    for _ in range(k):
        m = jnp.max(x, axis=0, keepdims=True)
        am = jnp.min(jnp.where(x == m, rid, float(n)), axis=0, keepdims=True)
        vals.append(m)
        ids.append(am)
        x = jnp.where(rid == am, -jnp.inf, x)
    return jnp.concatenate(vals, axis=0), jnp.concatenate(ids, axis=0)


def _head_topk(sim1, sim2):
    k = PEER_TOPK
    t = sim1.shape[1]
    jrow8 = lax.broadcasted_iota(I32, (SUBLANES, t), 0)
    (s1, i1), (s2, i2) = _topk_rows(sim1, k), _topk_rows(sim2, k)
    cs = [s1[0:1] + s2]
    ci = [i1[0:1] * N_KEYS + i2]
    for i in range(1, SUBLANES):
        nvalid = k // (i + 1)
        cs.append(jnp.where(jrow8 < nvalid, s1[i:i + 1] + s2[0:SUBLANES], -jnp.inf))
        ci.append(i1[i:i + 1] * N_KEYS + i2[0:SUBLANES])
    cs.append(s1[SUBLANES:k] + s2[0:1])
    ci.append(i1[SUBLANES:k] * N_KEYS + i2[0:1])
    cand_s = jnp.concatenate(cs, axis=0)
    cand_i = jnp.concatenate(ci, axis=0)
    top_s, pos = _topk_rows(cand_s, k)
    prow = lax.broadcasted_iota(I32, cand_s.shape, 0).astype(F32)
    sel = [jnp.sum(jnp.where(prow == pos[r:r + 1], cand_i, 0.0), axis=0, keepdims=True) for r in range(k)]
    ex = jnp.exp(top_s - top_s[0:1])
    return jnp.concatenate(sel, axis=0), ex / jnp.sum(ex, axis=0, keepdims=True)


def _sim_kernel(x_ref, mod_ref, g2_ref, wq_ref, keys_ref, h_ref, sim_ref, *, n_heads):
    x = x_ref[...]
    shift, scale = mod_ref[3:4, :], mod_ref[4:5, :]
    h = (_rms(x) * g2_ref[...]) * (1.0 + scale) + shift
    tb = x.shape[0]
    for c in range(SUBLANES):
        h_ref[pl.ds(c, tb, stride=SUBLANES), :] = h[:, c * LANES:(c + 1) * LANES]
    q = jnp.dot(h.astype(BF16), wq_ref[...], preferred_element_type=F32)
    for hp in range(2 * n_heads):
        sim_ref[hp] = lax.dot_general(keys_ref[hp // 2, hp % 2].astype(BF16),
                                      q[:, hp * HEAD_DIM:(hp + 1) * HEAD_DIM].astype(BF16),
                                      (((1,), (1,)), ((), ())), preferred_element_type=F32)


def _sim_call(x1, mod8, g2, wq, keys):
    b, s, d = x1.shape
    n_heads = keys.shape[0]
    tb = SIM_TILE
    nblk = s // tb
    kern = functools.partial(_sim_kernel, n_heads=n_heads)
    return pl.pallas_call(
        kern,
        grid=(b, nblk),
        in_specs=[pl.BlockSpec((None, tb, d), lambda i, j: (i, j, 0)),
                  pl.BlockSpec((None, 8, d), lambda i, j: (i, 0, 0)),
                  _const_spec(g2.shape), _const_spec(wq.shape), _const_spec(keys.shape)],
        out_specs=[pl.BlockSpec((tb * SUBLANES, LANES), lambda i, j: (i * nblk + j, 0)),
                   pl.BlockSpec((None, 2 * n_heads, N_KEYS, tb), lambda i, j: (i * nblk + j, 0, 0, 0))],
        out_shape=[jax.ShapeDtypeStruct((b * s * SUBLANES, LANES), F32),
                   jax.ShapeDtypeStruct((b * nblk, 2 * n_heads, N_KEYS, tb), F32)],
        compiler_params=pltpu.CompilerParams(dimension_semantics=("arbitrary", "arbitrary"),
                                             vmem_limit_bytes=VMEM_LIMIT_MIX),
        name="peer_similarities",
    )(x1, mod8, g2, wq, keys)


def _pack_kernel(x_ref, o_ref, stage):
    r = x_ref.shape[0]
    for c in range(SUBLANES):
        stage[pl.ds(c, r, stride=SUBLANES), :] = x_ref[:, c * LANES:(c + 1) * LANES]
    o_ref[...] = pltpu.bitcast(stage[...].astype(BF16), I32)


def _pack_table(tab, *, rows=512):
    n, d = tab.shape
    assert d == SUBLANES * LANES and n % rows == 0
    return pl.pallas_call(
        _pack_kernel,
        grid=(n // rows,),
        in_specs=[pl.BlockSpec((rows, d), lambda i: (i, 0))],
        out_specs=pl.BlockSpec((rows * 4, LANES), lambda i: (i, 0)),
        out_shape=jax.ShapeDtypeStruct((n * 4, LANES), I32),
        scratch_shapes=[pltpu.VMEM((rows * SUBLANES, LANES), F32)],
        name="peer_table_pack",
    )(tab)


def _gather_tile(tab_ref, ibuf, slot, positions):
    pieces = [tab_ref[pl.ds(pl.multiple_of(ibuf[slot, u, k], 4), 4), :] for u, k in positions]
    return pltpu.bitcast(jnp.concatenate(pieces, axis=0), BF16)


def _stage_ids(idx_ref, ibuf, sem, first_token, slot, group):
    return pltpu.make_async_copy(idx_ref.at[pl.ds(first_token, group)], ibuf.at[slot], sem.at[slot])


def _staged_token_loop(idx_ref, ibuf, sem, tbp, group, open_trip):
    ngroups = tbp // group

    def stage(g, slot):
        return _stage_ids(idx_ref, ibuf, sem, pl.multiple_of(g * group, group), slot, group)

    for s in range(IDX_SLOTS - 1):
        stage(s, s).start()

    def trip(i, carry):
        per_token, per_group = open_trip(i)
        for s in range(IDX_SLOTS):
            g = i * IDX_SLOTS + s
            stage(g, s).wait()
            stage(jnp.minimum(g + IDX_SLOTS - 1, ngroups - 1), (s + IDX_SLOTS - 1) % IDX_SLOTS).start()
            if per_group is not None:
                per_group(s)
            for u in range(group):
                per_token(s * group + u, s, u)
        return carry

    lax.fori_loop(0, ngroups // IDX_SLOTS, trip, 0)
    for s in range(IDX_SLOTS - 1):
        stage(ngroups - 1, s).wait()


def _window(ref, first, n):
    return ref.at[pl.ds(pl.multiple_of(first, SUBLANES), n)]


def _split_bf16(x, parts):
    out = []
    for _ in range(parts - 1):
        p = x.astype(BF16)
        out.append(p)
        x = x - p.astype(F32)
    out.append(x.astype(BF16))
    return out


def _sel_order(q):
    return (q % SUBLANES) * PEER_TOPK + q // SUBLANES


def _peer_u_kernel(sim_ref, h_ref, tab_ref, bsel_ref, act_ref, idx_ref, ibuf, sem, f_s, idt_s, gt_s, ids_s, gate_s,
                   *, tbp, nsel):
    trip = U_GROUP * IDX_SLOTS
    n_heads = nsel // PEER_TOPK
    units_per_group = (tbp // SIM_TILE) * n_heads * U_GROUP // tbp
    assert units_per_group * (tbp // U_GROUP) == (tbp // SIM_TILE) * n_heads

    @pl.when(pl.program_id(0) == 0)
    def _():
        ids_s[...] = jnp.zeros_like(ids_s)
        gate_s[...] = jnp.zeros_like(gate_s)

    idx_ref[...] = ids_s[...]

    def route_unit(w):
        tile, hh = w // n_heads, w % n_heads
        ids, gates = _head_topk(sim_ref[tile, 2 * hh], sim_ref[tile, 2 * hh + 1])
        idt_s[tile, pl.ds(pl.multiple_of(hh * PEER_TOPK, PEER_TOPK), PEER_TOPK), :] = ids * 4.0
        for r in range(PEER_TOPK):
            gt_s[tile, pl.ds(r * n_heads + hh, 1), :] = gates[r:r + 1]

    sub = lax.broadcasted_iota(I32, (SUBLANES, LANES), 0)
    lane = lax.broadcasted_iota(I32, (SUBLANES, LANES), 1)
    diag = sub == (lane & (SUBLANES - 1))
    m4 = (sub & 4) == 0
    m2 = (sub & 2) == 0
    m1 = (sub & 1) == 0

    def fold(a, b, dist, mask):
        return jnp.where(mask, a + pltpu.roll(a, SUBLANES - dist, 0), b + pltpu.roll(b, dist, 0))

    def column_sums(p):
        a0, a1 = fold(p[0], p[4], 4, m4), fold(p[2], p[6], 4, m4)
        a2, a3 = fold(p[1], p[5], 4, m4), fold(p[3], p[7], 4, m4)
        return fold(fold(a0, a1, 2, m2), fold(a2, a3, 2, m2), 1, m1)

    def partial_scores(h_win, tt, slot, u):
        hv = h_win[tt * SUBLANES:(tt + 1) * SUBLANES, :]
        h16 = jnp.concatenate(_split_bf16(hv, 2), axis=0)
        z = []
        for j in range(nsel // SEL_TILE):
            m = _gather_tile(tab_ref, ibuf, slot, [(u, j * SEL_TILE + i) for i in range(SEL_TILE)])
            o = lax.dot_general(h16, m, (((1,), (1,)), ((), ())), preferred_element_type=F32)
            o = o[0:SUBLANES] + o[SUBLANES:2 * SUBLANES]
            z += [jnp.where(diag, o[:, c * LANES:(c + 1) * LANES], 0.0) for c in range(SEL_TILE * SUBLANES // LANES)]
        f_s[tt * SUBLANES:(tt + 1) * SUBLANES, :] = column_sums(z)

    def finish_trip(act_win, gate_win):
        acc = None
        for v in range(SUBLANES):
            xv = f_s[pl.ds(v, trip, stride=SUBLANES), :]
            d = jnp.dot(jnp.concatenate(_split_bf16(xv, 3), axis=0), bsel_ref[v], preferred_element_type=F32)
            acc = d if acc is None else acc + d
        s = acc[0:trip] + acc[trip:2 * trip] + acc[2 * trip:3 * trip]
        act_win[...] = _gelu(s) * gate_win[...]

    f_s[...] = jnp.zeros_like(f_s)

    def open_trip(i):
        tok0 = i * trip
        prev0 = jnp.maximum(tok0 - trip, 0)
        finish_trip(_window(act_ref, prev0, trip), _window(gate_s, prev0, trip))
        h_win = _window(h_ref, tok0 * SUBLANES, trip * SUBLANES)

        def per_group(s):
            for n in range(units_per_group):
                route_unit((i * IDX_SLOTS + s) * units_per_group + n)

        return functools.partial(partial_scores, h_win), per_group

    _staged_token_loop(ids_s, ibuf, sem, tbp, U_GROUP, open_trip)
    finish_trip(_window(act_ref, tbp - trip, trip), _window(gate_s, tbp - trip, trip))
    for tile in range(tbp // SIM_TILE):
        tok = slice(tile * SIM_TILE, (tile + 1) * SIM_TILE)
        ids_s[tok, :] = idt_s[tile].T.astype(I32)
        gate_s[tok, :] = gt_s[tile].T


def _peer_v_kernel(idx_ref, act_ref, tab_ref, rep_ref, y_ref, ibuf, sem, arep_s, *, tbp, nsel):
    trip = V_GROUP * IDX_SLOTS
    width = nsel * SUBLANES
    diag = (lax.broadcasted_iota(I32, (SUBLANES, width), 0)
            == (lax.broadcasted_iota(I32, (SUBLANES, width), 1) & (SUBLANES - 1)))

    def open_trip(i):
        tok0 = i * trip
        parts = jnp.concatenate(_split_bf16(act_ref[pl.ds(pl.multiple_of(tok0, trip), trip), :], ACT_PARTS), axis=0)
        arep_s[...] = jnp.dot(parts, rep_ref[...], preferred_element_type=F32)
        y_win = _window(y_ref, tok0 * SUBLANES, trip * SUBLANES)

        def per_token(tt, slot, u):
            lhs = jnp.concatenate(
                [jnp.where(diag, jnp.broadcast_to(arep_s[p * trip + tt:p * trip + tt + 1, :], (SUBLANES, width)), 0.0)
                 for p in range(ACT_PARTS)], axis=0).astype(BF16)
            m = _gather_tile(tab_ref, ibuf, slot, [(u, _sel_order(q)) for q in range(nsel)])
            acc = jnp.dot(lhs, m, preferred_element_type=F32)
            y = acc[0:SUBLANES]
            for p in range(1, ACT_PARTS):
                y = y + acc[p * SUBLANES:(p + 1) * SUBLANES]
            y_win[tt * SUBLANES:(tt + 1) * SUBLANES, :] = y

        return per_token, None

    _staged_token_loop(idx_ref, ibuf, sem, tbp, V_GROUP, open_trip)


def _table_spec(shape):
    return pl.BlockSpec(shape, lambda i: (0, 0), pipeline_mode=pl.Buffered(1))


def _stage_scratch(nsel, group):
    return [pltpu.SMEM((IDX_SLOTS, group, nsel), I32), pltpu.SemaphoreType.DMA((IDX_SLOTS,))]


def _peer_u_call(sims, h8, tab, *, tbp):
    t = h8.shape[0] // SUBLANES
    nsel = (sims.shape[1] // 2) * PEER_TOPK
    nblk = t // tbp
    tiles = tbp // SIM_TILE
    assert sims.shape[3] == SIM_TILE and tbp % SIM_TILE == 0

    def routed(i):
        return jnp.minimum(i, nblk - 1)

    def scored(i):
        return jnp.maximum(i - 1, 0)

    kern = functools.partial(_peer_u_kernel, tbp=tbp, nsel=nsel)
    lane = jnp.arange(LANES)
    bsel = (lane[None, None, :] == (SUBLANES * (lane // SUBLANES))[None, :, None]
            + jnp.arange(SUBLANES)[:, None, None]).astype(BF16)
    return pl.pallas_call(
        kern,
        grid=(nblk + 1,),
        in_specs=[pl.BlockSpec((tiles,) + sims.shape[1:], lambda i: (routed(i), 0, 0, 0)),
                  pl.BlockSpec((tbp * SUBLANES, LANES), lambda i: (scored(i), 0)),
                  _table_spec(tab.shape),
                  pl.BlockSpec(bsel.shape, lambda i: (0, 0, 0))],
        out_specs=[pl.BlockSpec((tbp, nsel), lambda i: (scored(i), 0)),
                   pl.BlockSpec((tbp, nsel), lambda i: (scored(i), 0))],
        out_shape=[jax.ShapeDtypeStruct((t, nsel), F32), jax.ShapeDtypeStruct((t, nsel), I32)],
        scratch_shapes=_stage_scratch(nsel, U_GROUP) + [
            pltpu.VMEM((U_GROUP * IDX_SLOTS * SUBLANES, LANES), F32),
            pltpu.VMEM((tiles, nsel, SIM_TILE), F32), pltpu.VMEM((tiles, nsel, SIM_TILE), F32),
            pltpu.VMEM((tbp, nsel), I32), pltpu.VMEM((tbp, nsel), F32)],
        compiler_params=pltpu.CompilerParams(dimension_semantics=("arbitrary",),
                                             vmem_limit_bytes=VMEM_LIMIT_PEER),
        name="peer_route_scores",
    )(sims, h8, tab, bsel)


def _peer_v_call(idx4, act, tab, *, tbp):
    t, nsel = act.shape
    kern = functools.partial(_peer_v_kernel, tbp=tbp, nsel=nsel)
    rep = jnp.repeat(jnp.eye(nsel, dtype=BF16), SUBLANES, axis=1)
    return pl.pallas_call(
        kern,
        grid=(t // tbp,),
        in_specs=[pl.BlockSpec((tbp, nsel), lambda i: (i, 0)),
                  pl.BlockSpec((tbp, nsel), lambda i: (i, 0)),
                  _table_spec(tab.shape),
                  pl.BlockSpec(rep.shape, lambda i: (0, 0))],
        out_specs=pl.BlockSpec((tbp * SUBLANES, LANES), lambda i: (i, 0)),
        out_shape=jax.ShapeDtypeStruct((t * SUBLANES, LANES), F32),
        scratch_shapes=_stage_scratch(nsel, V_GROUP)
                       + [pltpu.VMEM((ACT_PARTS * V_GROUP * IDX_SLOTS, nsel * SUBLANES), F32)],
        compiler_params=pltpu.CompilerParams(dimension_semantics=("arbitrary",),
                                             vmem_limit_bytes=VMEM_LIMIT_PEER),
        name="peer_expert_mix",
    )(idx4, act, tab, rep)


def _resid_kernel(x_ref, y_ref, mod_ref, g_ref, o_ref, *, final):
    tb = x_ref.shape[0]
    y = jnp.concatenate([y_ref[pl.ds(c, tb, stride=SUBLANES), :] for c in range(SUBLANES)], axis=-1)
    z = x_ref[...] + mod_ref[5:6, :] * y
    o_ref[...] = _rms(z) * g_ref[...] if final else z


def _resid_call(x1, y, mod8, g, *, tb, final):
    b, s, d = x1.shape
    nblk = s // tb
    return pl.pallas_call(
        functools.partial(_resid_kernel, final=final),
        grid=(b, nblk),
        in_specs=[pl.BlockSpec((None, tb, d), lambda i, j: (i, j, 0)),
                  pl.BlockSpec((tb * SUBLANES, LANES), lambda i, j: (i * nblk + j, 0)),
                  pl.BlockSpec((None, 8, d), lambda i, j: (i, 0, 0)),
                  _const_spec(g.shape)],
        out_specs=pl.BlockSpec((None, tb, d), lambda i, j: (i, j, 0)),
        out_shape=jax.ShapeDtypeStruct((b, s, d), F32),
        name="peer_residual_norm",
    )(x1, y, mod8, g)


def _pick_block(n, pref):
    while n % pref:
        pref //= 2
    return pref


def kernel(x, c, ada_w, ada_b, norm1_g, w_in, lb_gamma, hgrn_norm_g, gmlp_ln_g, gmlp_ln_b, spatial_w,
           spatial_b, gmlp_norm_g, w_out, norm2_g, peer_wq, peer_keys, peer_u, peer_v, final_g):
    b, s, d = x.shape
    depth = ada_w.shape[0]
    t = b * s
    lower_bounds = jnp.cumsum(jax.nn.softmax(lb_gamma.astype(F32), axis=0), axis=0)
    c_pad = jnp.zeros((8, d), F32).at[:b].set(c)
    tb_mix = _pick_block(s, 256)
    tb_res = _pick_block(s, 512)
    tbp = _pick_block(t, 512)
    assert tbp % (max(U_GROUP, V_GROUP) * IDX_SLOTS) == 0
    for l in range(depth):
        mod = _mod_call(c_pad, ada_w[l], ada_b[l])[:b]
        mod8 = jnp.concatenate([mod.reshape(b, 6, d), jnp.zeros((b, 2, d), F32)], axis=1)
        x1 = _mix_call(x, mod8, norm1_g[l][None], w_in[l].astype(BF16), lower_bounds[l][None],
                       hgrn_norm_g[l][None], gmlp_ln_g[l][None], gmlp_ln_b[l][None], gmlp_norm_g[l][None],
                       spatial_w[l], spatial_b[l].T, w_out[l].astype(BF16), tb=tb_mix)
        h2, sims = _sim_call(x1, mod8, norm2_g[l][None], peer_wq[l].astype(BF16), peer_keys[l])
        act, idx4 = _peer_u_call(sims, h2, _pack_table(peer_u[l]), tbp=tbp)
        y8 = _peer_v_call(idx4, act, _pack_table(peer_v[l]), tbp=tbp)
        last = l == depth - 1
        x = _resid_call(x1, y8, mod8, final_g[None] if last else jnp.ones((1, d), F32),
                        tb=tb_res, final=last)
    return x
```

```python
import functools

import jax
import jax.numpy as jnp
from jax import lax
from jax.experimental import pallas as pl
from jax.experimental.pallas import tpu as pltpu

F32 = jnp.float32
BF16 = jnp.bfloat16
I32 = jnp.int32
HIGHEST = lax.Precision.HIGHEST
NORM_EPS = 1e-6

LANES = 128
SUBLANES = 8
HEAD_DIM = 128
GMLP_CHUNK = 128
SUB = 16
PEER_TOPK = 16
N_KEYS = 128
IDX_SLOTS = 2
U_GROUP = 32
V_GROUP = 64
SIDE_STEPS_PER_TOKEN = 2
SIM_TILE = 256
SEL_TILE = 32
ACT_PARTS = 2
VMEM_LIMIT_MIX = 48 * 1024 * 1024
VMEM_LIMIT_PEER = 52 * 1024 * 1024


def _gelu(x):
    return 0.5 * x * (1.0 + jnp.tanh(0.7978845608028654 * (x + 0.044715 * (x * x * x))))


def _sigmoid(x):
    return 1.0 / (1.0 + jnp.exp(-x))


def _rms(x, eps=NORM_EPS):
    return x * lax.rsqrt(jnp.mean(x * x, axis=-1, keepdims=True) + eps)


def _const_spec(shape):
    nd = len(shape)
    return pl.BlockSpec(shape, lambda *_: (0,) * nd)


def _mod_kernel(c_ref, w_ref, b_ref, o_ref):
    c = c_ref[...]
    ca = c * _sigmoid(c)
    o_ref[...] = jnp.dot(ca, w_ref[...], precision=HIGHEST, preferred_element_type=F32) + b_ref[...]


def _mod_call(c_pad, w, b):
    rows, d = c_pad.shape
    n = w.shape[1]
    bn = 1536 if n % 1536 == 0 else n
    return pl.pallas_call(
        _mod_kernel,
        grid=(n // bn,),
        in_specs=[pl.BlockSpec((rows, d), lambda j: (0, 0)),
                  pl.BlockSpec((d, bn), lambda j: (0, j)),
                  pl.BlockSpec((1, bn), lambda j: (0, j))],
        out_specs=pl.BlockSpec((rows, bn), lambda j: (0, j)),
        out_shape=jax.ShapeDtypeStruct((rows, n), F32),
        name="adaln_mod",
    )(c_pad, w, b.reshape(1, n))


def _mix_kernel(x_ref, mod_ref, g1_ref, win_ref, lb_ref, hg_ref, lng_ref, lnb_ref, gng_ref,
                sw_ref, sbt_ref, wout_ref, o_ref,
                state, qe_s, ke_s, qf_s, kg_s, bb_s, vv_s, cat_s, *, tb, hw, gw):
    n_heads = hw // HEAD_DIM
    n_groups = gw // HEAD_DIM

    @pl.when(pl.program_id(1) == 0)
    def _():
        state[...] = jnp.zeros_like(state)

    x = x_ref[...]
    shift, scale, gate = mod_ref[0:1, :], mod_ref[1:2, :], mod_ref[2:3, :]
    h = (_rms(x) * g1_ref[...]) * (1.0 + scale) + shift
    proj = jnp.dot(h.astype(BF16), win_ref[...], preferred_element_type=F32)

    lb = lb_ref[...]
    q = proj[:, 0:hw]
    fz = proj[:, hw:2 * hw]
    e = jnp.exp(-jnp.abs(fz))
    r = 1.0 / (1.0 + e)
    pos = fz >= 0.0
    sig = jnp.where(pos, r, e * r)
    sig_neg = jnp.where(pos, e * r, r)
    logf = jnp.log(lb + (1.0 - lb) * sig)
    kg = (1.0 - lb) * sig_neg
    qf = q * _sigmoid(q)
    ri = lax.broadcasted_iota(I32, (tb, tb), 0)
    ci = lax.broadcasted_iota(I32, (tb, tb), 1)
    same = (ri // SUB) == (ci // SUB)
    tri_rows = jnp.where(same & (ci <= ri), 1.0, 0.0).astype(BF16)
    tri = jnp.concatenate([tri_rows, jnp.where(same & (ci > ri), 1.0, 0.0).astype(BF16)], axis=0)
    sums = sum(jnp.dot(tri, part, preferred_element_type=F32) for part in _split_bf16(logf, 3))
    bb = sums[0:tb]
    rem = sums[tb:2 * tb]
    qe_s[...] = qf * jnp.exp(bb)
    ke_s[...] = kg * jnp.exp(rem)
    qf_s[...] = qf
    kg_s[...] = kg
    bb_s[...] = bb
    vv_s[...] = proj[:, 2 * hw:3 * hw]

    rows16 = lax.broadcasted_iota(I32, (SUB, HEAD_DIM), 0)
    rows128 = lax.broadcasted_iota(I32, (GMLP_CHUNK, HEAD_DIM), 0)
    steps_per_chunk = GMLP_CHUNK // SUB

    def chunk_body(c, carry):
        c0 = pl.multiple_of(c * GMLP_CHUNK, GMLP_CHUNK)
        for hd in range(n_heads):
            cs = slice(hd * HEAD_DIM, (hd + 1) * HEAD_DIM)
            ke_c = ke_s[pl.ds(c0, GMLP_CHUNK), cs]
            v_t = vv_s[pl.ds(c0, GMLP_CHUNK), cs].T.astype(BF16)
            st = state[hd]
            outs = []
            for j in range(steps_per_chunk):
                rs = pl.ds(c0 + j * SUB, SUB)
                bb_b, qf_b, kg_b, v_b = bb_s[rs, cs], qf_s[rs, cs], kg_s[rs, cs], vv_s[rs, cs]
                o = lax.dot_general(qe_s[rs, cs].astype(BF16), st.astype(BF16), (((1,), (1,)), ((), ())),
                                    preferred_element_type=F32)
                for s in range(SUB):
                    dec = jnp.where(rows16 >= s, jnp.exp(bb_b - bb_b[s:s + 1, :]), 0.0)
                    w = jnp.sum(qf_b * dec * kg_b[s:s + 1, :], axis=-1, keepdims=True)
                    o = o + w * v_b[s:s + 1, :]
                outs.append(o)
                in_step = (rows128 >= j * SUB) & (rows128 < (j + 1) * SUB)
                ke_m = jnp.where(in_step, ke_c, 0.0).astype(BF16)
                upd = jnp.dot(v_t, ke_m, preferred_element_type=F32)
                st = jnp.exp(bb_b[SUB - 1:SUB, :]) * st + upd
            state[hd] = st
            o_c = jnp.concatenate(outs, axis=0)
            cat_s[pl.ds(c0, GMLP_CHUNK), cs] = _rms(o_c) * hg_ref[:, cs]
        return carry

    lax.fori_loop(0, tb // GMLP_CHUNK, chunk_body, 0)

    og = proj[:, 3 * hw:4 * hw]
    o_all = cat_s[:, 0:hw] * (og * _sigmoid(og))

    u = _gelu(proj[:, 4 * hw:4 * hw + gw])
    v = _gelu(proj[:, 4 * hw + gw:4 * hw + 2 * gw])
    mu = jnp.mean(v, axis=-1, keepdims=True)
    vc = v - mu
    var = jnp.mean(vc * vc, axis=-1, keepdims=True)
    vln = (vc * lax.rsqrt(var + NORM_EPS)) * lng_ref[...] + lnb_ref[...]
    tri = (lax.broadcasted_iota(I32, (GMLP_CHUNK, GMLP_CHUNK), 1)
           <= lax.broadcasted_iota(I32, (GMLP_CHUNK, GMLP_CHUNK), 0))
    gm_groups = []
    for g in range(n_groups):
        gs = slice(g * HEAD_DIM, (g + 1) * HEAD_DIM)
        wm = jnp.where(tri, sw_ref[g], 0.0).astype(BF16)
        bias = sbt_ref[:, g:g + 1]
        parts = []
        for cidx in range(tb // GMLP_CHUNK):
            rs = slice(cidx * GMLP_CHUNK, (cidx + 1) * GMLP_CHUNK)
            mixed = jnp.dot(wm, vln[rs, gs].astype(BF16), preferred_element_type=F32) + bias
            parts.append(u[rs, gs] * mixed)
        gmg = jnp.concatenate(parts, axis=0) if len(parts) > 1 else parts[0]
        gm_groups.append(_rms(gmg) * gng_ref[:, gs])
    gm = jnp.concatenate(gm_groups, axis=-1)

    cat = jnp.concatenate([o_all, gm], axis=-1).astype(BF16)
    mixed_out = jnp.dot(cat, wout_ref[...], preferred_element_type=F32)
    o_ref[...] = x + gate * mixed_out


def _mix_call(x, mod8, g1, w_in, lb, hg, lng, lnb, gng, sw, sbt, w_out, *, tb):
    b, s, d = x.shape
    hw = lb.shape[1]
    gw = lng.shape[1]
    n_heads = hw // HEAD_DIM
    kern = functools.partial(_mix_kernel, tb=tb, hw=hw, gw=gw)
    return pl.pallas_call(
        kern,
        grid=(b, s // tb),
        in_specs=[pl.BlockSpec((None, tb, d), lambda i, j: (i, j, 0)),
                  pl.BlockSpec((None, 8, d), lambda i, j: (i, 0, 0)),
                  _const_spec(g1.shape), _const_spec(w_in.shape), _const_spec(lb.shape),
                  _const_spec(hg.shape), _const_spec(lng.shape), _const_spec(lnb.shape),
                  _const_spec(gng.shape), _const_spec(sw.shape), _const_spec(sbt.shape),
                  _const_spec(w_out.shape)],
        out_specs=pl.BlockSpec((None, tb, d), lambda i, j: (i, j, 0)),
        out_shape=jax.ShapeDtypeStruct((b, s, d), F32),
        scratch_shapes=[pltpu.VMEM((n_heads, HEAD_DIM, HEAD_DIM), F32)]
                       + [pltpu.VMEM((tb, hw), F32) for _ in range(7)],
        compiler_params=pltpu.CompilerParams(dimension_semantics=("arbitrary", "arbitrary"),
                                             vmem_limit_bytes=VMEM_LIMIT_MIX),
        name="hgrn_gmlp_mixer",
    )(x, mod8, g1, w_in, lb, hg, lng, lnb, gng, sw, sbt, w_out)


def _topk_rows(x, k, out):
    n = x.shape[0]
    rid = lax.broadcasted_iota(I32, x.shape, 0).astype(F32)
    vals, ids = [], []
    for _ in range(k):
        m = jnp.max(x, axis=0, keepdims=True)
        am = jnp.min(jnp.where(x == m, rid, float(n)), axis=0, keepdims=True)
        vals.append(m)
        ids.append(am)
        x = jnp.where(rid == am, -jnp.inf, x)
        yield
    out.append((jnp.concatenate(vals, axis=0), jnp.concatenate(ids, axis=0)))


def _head_topk(load_sim1, load_sim2, out):
    k = PEER_TOPK
    tops = []
    yield from _topk_rows(load_sim1(), k, tops)
    yield from _topk_rows(load_sim2(), k, tops)
    (s1, i1), (s2, i2) = tops
    t = s1.shape[1]
    jrow8 = lax.broadcasted_iota(I32, (SUBLANES, t), 0)
    cs = [s1[0:1] + s2]
    ci = [i1[0:1] * N_KEYS + i2]
    for i in range(1, SUBLANES):
        nvalid = k // (i + 1)
        cs.append(jnp.where(jrow8 < nvalid, s1[i:i + 1] + s2[0:SUBLANES], -jnp.inf))
        ci.append(i1[i:i + 1] * N_KEYS + i2[0:SUBLANES])
    cs.append(s1[SUBLANES:k] + s2[0:1])
    ci.append(i1[SUBLANES:k] * N_KEYS + i2[0:1])
    cand_s = jnp.concatenate(cs, axis=0)
    cand_i = jnp.concatenate(ci, axis=0)
    best = []
    yield from _topk_rows(cand_s, k, best)
    top_s, pos = best[0]
    prow = lax.broadcasted_iota(I32, cand_s.shape, 0).astype(F32)
    sel = [jnp.sum(jnp.where(prow == pos[r:r + 1], cand_i, 0.0), axis=0, keepdims=True) for r in range(k)]
    ex = jnp.exp(top_s - top_s[0:1])
    out.append((jnp.concatenate(sel, axis=0), ex / jnp.sum(ex, axis=0, keepdims=True)))


def _sim_kernel(x_ref, mod_ref, g2_ref, wq_ref, keys_ref, h_ref, sim_ref, *, n_heads):
    x = x_ref[...]
    shift, scale = mod_ref[3:4, :], mod_ref[4:5, :]
    h = (_rms(x) * g2_ref[...]) * (1.0 + scale) + shift
    tb = x.shape[0]
    for c in range(SUBLANES):
        h_ref[pl.ds(c, tb, stride=SUBLANES), :] = h[:, c * LANES:(c + 1) * LANES]
    q = jnp.dot(h.astype(BF16), wq_ref[...], preferred_element_type=F32)
    for hp in range(2 * n_heads):
        sim_ref[hp] = lax.dot_general(keys_ref[hp // 2, hp % 2].astype(BF16),
                                      q[:, hp * HEAD_DIM:(hp + 1) * HEAD_DIM].astype(BF16),
                                      (((1,), (1,)), ((), ())), preferred_element_type=F32)


def _sim_call(x1, mod8, g2, wq, keys):
    b, s, d = x1.shape
    n_heads = keys.shape[0]
    tb = SIM_TILE
    nblk = s // tb
    kern = functools.partial(_sim_kernel, n_heads=n_heads)
    return pl.pallas_call(
        kern,
        grid=(b, nblk),
        in_specs=[pl.BlockSpec((None, tb, d), lambda i, j: (i, j, 0)),
                  pl.BlockSpec((None, 8, d), lambda i, j: (i, 0, 0)),
                  _const_spec(g2.shape), _const_spec(wq.shape), _const_spec(keys.shape)],
        out_specs=[pl.BlockSpec((tb * SUBLANES, LANES), lambda i, j: (i * nblk + j, 0)),
                   pl.BlockSpec((None, 2 * n_heads, N_KEYS, tb), lambda i, j: (i * nblk + j, 0, 0, 0))],
        out_shape=[jax.ShapeDtypeStruct((b * s * SUBLANES, LANES), F32),
                   jax.ShapeDtypeStruct((b * nblk, 2 * n_heads, N_KEYS, tb), F32)],
        compiler_params=pltpu.CompilerParams(dimension_semantics=("arbitrary", "arbitrary"),
                                             vmem_limit_bytes=VMEM_LIMIT_MIX),
        name="peer_similarities",
    )(x1, mod8, g2, wq, keys)


def _pack_kernel(x_ref, o_ref, stage):
    r = x_ref.shape[0]
    for c in range(SUBLANES):
        stage[pl.ds(c, r, stride=SUBLANES), :] = x_ref[:, c * LANES:(c + 1) * LANES]
    o_ref[...] = pltpu.bitcast(stage[...].astype(BF16), I32)


def _pack_table(tab, *, rows=512):
    n, d = tab.shape
    assert d == SUBLANES * LANES and n % rows == 0
    return pl.pallas_call(
        _pack_kernel,
        grid=(n // rows,),
        in_specs=[pl.BlockSpec((rows, d), lambda i: (i, 0))],
        out_specs=pl.BlockSpec((rows * 4, LANES), lambda i: (i, 0)),
        out_shape=jax.ShapeDtypeStruct((n * 4, LANES), I32),
        scratch_shapes=[pltpu.VMEM((rows * SUBLANES, LANES), F32)],
        name="peer_table_pack",
    )(tab)


def _gather_tile(tab_ref, ibuf, slot, positions):
    pieces = [tab_ref[pl.ds(pl.multiple_of(ibuf[slot, u, k], 4), 4), :] for u, k in positions]
    return pltpu.bitcast(jnp.concatenate(pieces, axis=0), BF16)


def _stage_ids(idx_ref, ibuf, sem, first_token, slot, group):
    return pltpu.make_async_copy(idx_ref.at[pl.ds(first_token, group)], ibuf.at[slot], sem.at[slot])


def _staged_token_loop(idx_ref, ibuf, sem, tbp, group, open_trip):
    ngroups = tbp // group

    def stage(g, slot):
        return _stage_ids(idx_ref, ibuf, sem, pl.multiple_of(g * group, group), slot, group)

    for s in range(IDX_SLOTS - 1):
        stage(s, s).start()

    def trip(i, carry):
        per_token, per_group = open_trip(i)
        for s in range(IDX_SLOTS):
            g = i * IDX_SLOTS + s
            stage(g, s).wait()
            stage(jnp.minimum(g + IDX_SLOTS - 1, ngroups - 1), (s + IDX_SLOTS - 1) % IDX_SLOTS).start()
            side = iter(()) if per_group is None else per_group(s)
            for u in range(group):
                per_token(s * group + u, s, u)
                for _ in range(SIDE_STEPS_PER_TOKEN):
                    next(side, None)
            for _ in side:
                pass
        return carry

    lax.fori_loop(0, ngroups // IDX_SLOTS, trip, 0)
    for s in range(IDX_SLOTS - 1):
        stage(ngroups - 1, s).wait()


def _window(ref, first, n):
    return ref.at[pl.ds(pl.multiple_of(first, SUBLANES), n)]


def _split_bf16(x, parts):
    out = []
    for _ in range(parts - 1):
        p = x.astype(BF16)
        out.append(p)
        x = x - p.astype(F32)
    out.append(x.astype(BF16))
    return out


def _sel_order(q):
    return (q % SUBLANES) * PEER_TOPK + q // SUBLANES


def _peer_u_kernel(sim_ref, h_ref, tab_ref, bsel_ref, act_ref, idx_ref, ibuf, sem, f_s, idt_s, gt_s, ids_s, gate_s,
                   *, tbp, nsel):
    trip = U_GROUP * IDX_SLOTS
    n_heads = nsel // PEER_TOPK
    units_per_group = (tbp // SIM_TILE) * n_heads * U_GROUP // tbp
    assert units_per_group * (tbp // U_GROUP) == (tbp // SIM_TILE) * n_heads

    @pl.when(pl.program_id(0) == 0)
    def _():
        ids_s[...] = jnp.zeros_like(ids_s)
        gate_s[...] = jnp.zeros_like(gate_s)

    idx_ref[...] = ids_s[...]

    def route_unit(w):
        tile, hh = w // n_heads, w % n_heads
        out = []
        yield from _head_topk(lambda: sim_ref[tile, 2 * hh], lambda: sim_ref[tile, 2 * hh + 1], out)
        ids, gates = out[0]
        idt_s[tile, pl.ds(pl.multiple_of(hh * PEER_TOPK, PEER_TOPK), PEER_TOPK), :] = ids * 4.0
        for r in range(PEER_TOPK):
            gt_s[tile, pl.ds(r * n_heads + hh, 1), :] = gates[r:r + 1]

    sub = lax.broadcasted_iota(I32, (SUBLANES, LANES), 0)
    lane = lax.broadcasted_iota(I32, (SUBLANES, LANES), 1)
    diag = sub == (lane & (SUBLANES - 1))
    m4 = (sub & 4) == 0
    m2 = (sub & 2) == 0
    m1 = (sub & 1) == 0

    def fold(a, b, dist, mask):
        return jnp.where(mask, a + pltpu.roll(a, SUBLANES - dist, 0), b + pltpu.roll(b, dist, 0))

    def column_sums(p):
        a0, a1 = fold(p[0], p[4], 4, m4), fold(p[2], p[6], 4, m4)
        a2, a3 = fold(p[1], p[5], 4, m4), fold(p[3], p[7], 4, m4)
        return fold(fold(a0, a1, 2, m2), fold(a2, a3, 2, m2), 1, m1)

    def partial_scores(h_win, tt, slot, u):
        hv = h_win[tt * SUBLANES:(tt + 1) * SUBLANES, :]
        h16 = jnp.concatenate(_split_bf16(hv, 2), axis=0)
        z = []
        for j in range(nsel // SEL_TILE):
            m = _gather_tile(tab_ref, ibuf, slot, [(u, j * SEL_TILE + i) for i in range(SEL_TILE)])
            o = lax.dot_general(h16, m, (((1,), (1,)), ((), ())), preferred_element_type=F32)
            o = o[0:SUBLANES] + o[SUBLANES:2 * SUBLANES]
            z += [jnp.where(diag, o[:, c * LANES:(c + 1) * LANES], 0.0) for c in range(SEL_TILE * SUBLANES // LANES)]
        f_s[tt * SUBLANES:(tt + 1) * SUBLANES, :] = column_sums(z)

    def finish_trip(act_win, gate_win):
        acc = None
        for v in range(SUBLANES):
            xv = f_s[pl.ds(v, trip, stride=SUBLANES), :]
            d = jnp.dot(jnp.concatenate(_split_bf16(xv, 3), axis=0), bsel_ref[v], preferred_element_type=F32)
            acc = d if acc is None else acc + d
        s = acc[0:trip] + acc[trip:2 * trip] + acc[2 * trip:3 * trip]
        act_win[...] = _gelu(s) * gate_win[...]

    f_s[...] = jnp.zeros_like(f_s)

    def open_trip(i):
        tok0 = i * trip
        prev0 = jnp.maximum(tok0 - trip, 0)
        finish_trip(_window(act_ref, prev0, trip), _window(gate_s, prev0, trip))
        h_win = _window(h_ref, tok0 * SUBLANES, trip * SUBLANES)

        def per_group(s):
            for n in range(units_per_group):
                yield from route_unit((i * IDX_SLOTS + s) * units_per_group + n)

        return functools.partial(partial_scores, h_win), per_group

    _staged_token_loop(ids_s, ibuf, sem, tbp, U_GROUP, open_trip)
    finish_trip(_window(act_ref, tbp - trip, trip), _window(gate_s, tbp - trip, trip))
    for tile in range(tbp // SIM_TILE):
        tok = slice(tile * SIM_TILE, (tile + 1) * SIM_TILE)
        ids_s[tok, :] = idt_s[tile].T.astype(I32)
        gate_s[tok, :] = gt_s[tile].T


def _peer_v_kernel(idx_ref, act_ref, tab_ref, rep_ref, y_ref, ibuf, sem, arep_s, *, tbp, nsel):
    trip = V_GROUP * IDX_SLOTS
    width = nsel * SUBLANES
    diag = (lax.broadcasted_iota(I32, (SUBLANES, width), 0)
            == (lax.broadcasted_iota(I32, (SUBLANES, width), 1) & (SUBLANES - 1)))

    def open_trip(i):
        tok0 = i * trip
        parts = jnp.concatenate(_split_bf16(act_ref[pl.ds(pl.multiple_of(tok0, trip), trip), :], ACT_PARTS), axis=0)
        arep_s[...] = jnp.dot(parts, rep_ref[...], preferred_element_type=F32)
        y_win = _window(y_ref, tok0 * SUBLANES, trip * SUBLANES)

        def per_token(tt, slot, u):
            lhs = jnp.concatenate(
                [jnp.where(diag, jnp.broadcast_to(arep_s[p * trip + tt:p * trip + tt + 1, :], (SUBLANES, width)), 0.0)
                 for p in range(ACT_PARTS)], axis=0).astype(BF16)
            m = _gather_tile(tab_ref, ibuf, slot, [(u, _sel_order(q)) for q in range(nsel)])
            acc = jnp.dot(lhs, m, preferred_element_type=F32)
            y = acc[0:SUBLANES]
            for p in range(1, ACT_PARTS):
                y = y + acc[p * SUBLANES:(p + 1) * SUBLANES]
            y_win[tt * SUBLANES:(tt + 1) * SUBLANES, :] = y

        return per_token, None

    _staged_token_loop(idx_ref, ibuf, sem, tbp, V_GROUP, open_trip)


def _table_spec(shape):
    return pl.BlockSpec(shape, lambda i: (0, 0), pipeline_mode=pl.Buffered(1))


def _stage_scratch(nsel, group):
    return [pltpu.SMEM((IDX_SLOTS, group, nsel), I32), pltpu.SemaphoreType.DMA((IDX_SLOTS,))]


def _peer_u_call(sims, h8, tab, *, tbp):
    t = h8.shape[0] // SUBLANES
    nsel = (sims.shape[1] // 2) * PEER_TOPK
    nblk = t // tbp
    tiles = tbp // SIM_TILE
    assert sims.shape[3] == SIM_TILE and tbp % SIM_TILE == 0

    def routed(i):
        return jnp.minimum(i, nblk - 1)

    def scored(i):
        return jnp.maximum(i - 1, 0)

    kern = functools.partial(_peer_u_kernel, tbp=tbp, nsel=nsel)
    lane = jnp.arange(LANES)
    bsel = (lane[None, None, :] == (SUBLANES * (lane // SUBLANES))[None, :, None]
            + jnp.arange(SUBLANES)[:, None, None]).astype(BF16)
    return pl.pallas_call(
        kern,
        grid=(nblk + 1,),
        in_specs=[pl.BlockSpec((tiles,) + sims.shape[1:], lambda i: (routed(i), 0, 0, 0)),
                  pl.BlockSpec((tbp * SUBLANES, LANES), lambda i: (scored(i), 0)),
                  _table_spec(tab.shape),
                  pl.BlockSpec(bsel.shape, lambda i: (0, 0, 0))],
        out_specs=[pl.BlockSpec((tbp, nsel), lambda i: (scored(i), 0)),
                   pl.BlockSpec((tbp, nsel), lambda i: (scored(i), 0))],
        out_shape=[jax.ShapeDtypeStruct((t, nsel), F32), jax.ShapeDtypeStruct((t, nsel), I32)],
        scratch_shapes=_stage_scratch(nsel, U_GROUP) + [
            pltpu.VMEM((U_GROUP * IDX_SLOTS * SUBLANES, LANES), F32),
            pltpu.VMEM((tiles, nsel, SIM_TILE), F32), pltpu.VMEM((tiles, nsel, SIM_TILE), F32),
            pltpu.VMEM((tbp, nsel), I32), pltpu.VMEM((tbp, nsel), F32)],
        compiler_params=pltpu.CompilerParams(dimension_semantics=("arbitrary",),
                                             vmem_limit_bytes=VMEM_LIMIT_PEER),
        name="peer_route_scores",
    )(sims, h8, tab, bsel)


def _peer_v_call(idx4, act, tab, *, tbp):
    t, nsel = act.shape
    kern = functools.partial(_peer_v_kernel, tbp=tbp, nsel=nsel)
    rep = jnp.repeat(jnp.eye(nsel, dtype=BF16), SUBLANES, axis=1)
    return pl.pallas_call(
        kern,
        grid=(t // tbp,),
        in_specs=[pl.BlockSpec((tbp, nsel), lambda i: (i, 0)),
                  pl.BlockSpec((tbp, nsel), lambda i: (i, 0)),
                  _table_spec(tab.shape),
                  pl.BlockSpec(rep.shape, lambda i: (0, 0))],
        out_specs=pl.BlockSpec((tbp * SUBLANES, LANES), lambda i: (i, 0)),
        out_shape=jax.ShapeDtypeStruct((t * SUBLANES, LANES), F32),
        scratch_shapes=_stage_scratch(nsel, V_GROUP)
                       + [pltpu.VMEM((ACT_PARTS * V_GROUP * IDX_SLOTS, nsel * SUBLANES), F32)],
        compiler_params=pltpu.CompilerParams(dimension_semantics=("arbitrary",),
                                             vmem_limit_bytes=VMEM_LIMIT_PEER),
        name="peer_expert_mix",
    )(idx4, act, tab, rep)


def _resid_kernel(x_ref, y_ref, mod_ref, g_ref, o_ref, *, final):
    tb = x_ref.shape[0]
    y = jnp.concatenate([y_ref[pl.ds(c, tb, stride=SUBLANES), :] for c in range(SUBLANES)], axis=-1)
    z = x_ref[...] + mod_ref[5:6, :] * y
    o_ref[...] = _rms(z) * g_ref[...] if final else z


def _resid_call(x1, y, mod8, g, *, tb, final):
    b, s, d = x1.shape
    nblk = s // tb
    return pl.pallas_call(
        functools.partial(_resid_kernel, final=final),
        grid=(b, nblk),
        in_specs=[pl.BlockSpec((None, tb, d), lambda i, j: (i, j, 0)),
                  pl.BlockSpec((tb * SUBLANES, LANES), lambda i, j: (i * nblk + j, 0)),
                  pl.BlockSpec((None, 8, d), lambda i, j: (i, 0, 0)),
                  _const_spec(g.shape)],
        out_specs=pl.BlockSpec((None, tb, d), lambda i, j: (i, j, 0)),
        out_shape=jax.ShapeDtypeStruct((b, s, d), F32),
        name="peer_residual_norm",
    )(x1, y, mod8, g)


def _pick_block(n, pref):
    while n % pref:
        pref //= 2
    return pref


def kernel(x, c, ada_w, ada_b, norm1_g, w_in, lb_gamma, hgrn_norm_g, gmlp_ln_g, gmlp_ln_b, spatial_w,
           spatial_b, gmlp_norm_g, w_out, norm2_g, peer_wq, peer_keys, peer_u, peer_v, final_g):
    b, s, d = x.shape
    depth = ada_w.shape[0]
    t = b * s
    lower_bounds = jnp.cumsum(jax.nn.softmax(lb_gamma.astype(F32), axis=0), axis=0)
    c_pad = jnp.zeros((8, d), F32).at[:b].set(c)
    tb_mix = _pick_block(s, 256)
    tb_res = _pick_block(s, 512)
    tbp = _pick_block(t, 512)
    assert tbp % (max(U_GROUP, V_GROUP) * IDX_SLOTS) == 0
    for l in range(depth):
        mod = _mod_call(c_pad, ada_w[l], ada_b[l])[:b]
        mod8 = jnp.concatenate([mod.reshape(b, 6, d), jnp.zeros((b, 2, d), F32)], axis=1)
        x1 = _mix_call(x, mod8, norm1_g[l][None], w_in[l].astype(BF16), lower_bounds[l][None],
                       hgrn_norm_g[l][None], gmlp_ln_g[l][None], gmlp_ln_b[l][None], gmlp_norm_g[l][None],
                       spatial_w[l], spatial_b[l].T, w_out[l].astype(BF16), tb=tb_mix)
        h2, sims = _sim_call(x1, mod8, norm2_g[l][None], peer_wq[l].astype(BF16), peer_keys[l])
        act, idx4 = _peer_u_call(sims, h2, _pack_table(peer_u[l]), tbp=tbp)
        y8 = _peer_v_call(idx4, act, _pack_table(peer_v[l]), tbp=tbp)
        last = l == depth - 1
        x = _resid_call(x1, y8, mod8, final_g[None] if last else jnp.ones((1, d), F32),
                        tb=tb_res, final=last)
    return x
```

```python
import functools

import jax
import jax.numpy as jnp
from jax import lax
from jax.experimental import pallas as pl
from jax.experimental.pallas import tpu as pltpu

F32 = jnp.float32
BF16 = jnp.bfloat16
I32 = jnp.int32
HIGHEST = lax.Precision.HIGHEST
NORM_EPS = 1e-6

LANES = 128
SUBLANES = 8
HEAD_DIM = 128
GMLP_CHUNK = 128
SUB = 16
PEER_TOPK = 16
N_KEYS = 128
ROW_WORDS = SUBLANES // 2
MOD_SHIFT1, MOD_SCALE1, MOD_GATE1, MOD_SHIFT2, MOD_SCALE2, MOD_GATE2 = range(6)
IDX_SLOTS = 2
U_GROUP = 32
V_GROUP = 64
SIDE_STEPS_PER_TOKEN = 2
SIM_TILE = 256
SEL_TILE = 32
ACT_PARTS = 2
VMEM_LIMIT_MIX = 48 * 1024 * 1024
VMEM_LIMIT_PEER = 52 * 1024 * 1024


def _gelu(x):
    return 0.5 * x * (1.0 + jnp.tanh(0.7978845608028654 * (x + 0.044715 * (x * x * x))))


def _sigmoid(x):
    return 1.0 / (1.0 + jnp.exp(-x))


def _rms(x, eps=NORM_EPS):
    return x * lax.rsqrt(jnp.mean(x * x, axis=-1, keepdims=True) + eps)


def _const_spec(shape):
    nd = len(shape)
    return pl.BlockSpec(shape, lambda *_: (0,) * nd)


def _mod_kernel(c_ref, w_ref, b_ref, o_ref):
    c = c_ref[...]
    ca = c * _sigmoid(c)
    o_ref[...] = jnp.dot(ca, w_ref[...], precision=HIGHEST, preferred_element_type=F32) + b_ref[...]


def _mod_call(c_pad, w, b):
    rows, d = c_pad.shape
    n = w.shape[1]
    bn = 1536 if n % 1536 == 0 else n
    return pl.pallas_call(
        _mod_kernel,
        grid=(n // bn,),
        in_specs=[pl.BlockSpec((rows, d), lambda j: (0, 0)),
                  pl.BlockSpec((d, bn), lambda j: (0, j)),
                  pl.BlockSpec((1, bn), lambda j: (0, j))],
        out_specs=pl.BlockSpec((rows, bn), lambda j: (0, j)),
        out_shape=jax.ShapeDtypeStruct((rows, n), F32),
        name="adaln_mod",
    )(c_pad, w, b.reshape(1, n))


def _mix_kernel(x_ref, mod_ref, g1_ref, win_ref, lb_ref, hg_ref, lng_ref, lnb_ref, gng_ref,
                sw_ref, sbt_ref, wout_ref, o_ref,
                state, qe_s, ke_s, qf_s, kg_s, bb_s, vv_s, cat_s, *, tb, hw, gw):
    n_heads = hw // HEAD_DIM
    n_groups = gw // HEAD_DIM

    @pl.when(pl.program_id(1) == 0)
    def _():
        state[...] = jnp.zeros_like(state)

    x = x_ref[...]
    shift, scale, gate = (mod_ref[r:r + 1, :] for r in (MOD_SHIFT1, MOD_SCALE1, MOD_GATE1))
    h = (_rms(x) * g1_ref[...]) * (1.0 + scale) + shift
    proj = jnp.dot(h.astype(BF16), win_ref[...], preferred_element_type=F32)

    lb = lb_ref[...]
    q = proj[:, 0:hw]
    fz = proj[:, hw:2 * hw]
    e = jnp.exp(-jnp.abs(fz))
    r = 1.0 / (1.0 + e)
    pos = fz >= 0.0
    sig = jnp.where(pos, r, e * r)
    sig_neg = jnp.where(pos, e * r, r)
    logf = jnp.log(lb + (1.0 - lb) * sig)
    kg = (1.0 - lb) * sig_neg
    qf = q * _sigmoid(q)
    ri = lax.broadcasted_iota(I32, (tb, tb), 0)
    ci = lax.broadcasted_iota(I32, (tb, tb), 1)
    same = (ri // SUB) == (ci // SUB)
    tri_rows = jnp.where(same & (ci <= ri), 1.0, 0.0).astype(BF16)
    tri = jnp.concatenate([tri_rows, jnp.where(same & (ci > ri), 1.0, 0.0).astype(BF16)], axis=0)
    sums = sum(jnp.dot(tri, part, preferred_element_type=F32) for part in _split_bf16(logf, 3))
    bb = sums[0:tb]
    rem = sums[tb:2 * tb]
    qe_s[...] = qf * jnp.exp(bb)
    ke_s[...] = kg * jnp.exp(rem)
    qf_s[...] = qf
    kg_s[...] = kg
    bb_s[...] = bb
    vv_s[...] = proj[:, 2 * hw:3 * hw]

    rows16 = lax.broadcasted_iota(I32, (SUB, HEAD_DIM), 0)
    rows128 = lax.broadcasted_iota(I32, (GMLP_CHUNK, HEAD_DIM), 0)
    steps_per_chunk = GMLP_CHUNK // SUB

    def chunk_body(c, carry):
        c0 = pl.multiple_of(c * GMLP_CHUNK, GMLP_CHUNK)
        for hd in range(n_heads):
            cs = slice(hd * HEAD_DIM, (hd + 1) * HEAD_DIM)
            ke_c = ke_s[pl.ds(c0, GMLP_CHUNK), cs]
            v_t = vv_s[pl.ds(c0, GMLP_CHUNK), cs].T.astype(BF16)
            st = state[hd]
            outs = []
            for j in range(steps_per_chunk):
                rs = pl.ds(c0 + j * SUB, SUB)
                bb_b, qf_b, kg_b, v_b = bb_s[rs, cs], qf_s[rs, cs], kg_s[rs, cs], vv_s[rs, cs]
                o = lax.dot_general(qe_s[rs, cs].astype(BF16), st.astype(BF16), (((1,), (1,)), ((), ())),
                                    preferred_element_type=F32)
                for s in range(SUB):
                    dec = jnp.where(rows16 >= s, jnp.exp(bb_b - bb_b[s:s + 1, :]), 0.0)
                    w = jnp.sum(qf_b * dec * kg_b[s:s + 1, :], axis=-1, keepdims=True)
                    o = o + w * v_b[s:s + 1, :]
                outs.append(o)
                in_step = (rows128 >= j * SUB) & (rows128 < (j + 1) * SUB)
                ke_m = jnp.where(in_step, ke_c, 0.0).astype(BF16)
                upd = jnp.dot(v_t, ke_m, preferred_element_type=F32)
                st = jnp.exp(bb_b[SUB - 1:SUB, :]) * st + upd
            state[hd] = st
            o_c = jnp.concatenate(outs, axis=0)
            cat_s[pl.ds(c0, GMLP_CHUNK), cs] = _rms(o_c) * hg_ref[:, cs]
        return carry

    lax.fori_loop(0, tb // GMLP_CHUNK, chunk_body, 0)

    og = proj[:, 3 * hw:4 * hw]
    o_all = cat_s[:, 0:hw] * (og * _sigmoid(og))

    u = _gelu(proj[:, 4 * hw:4 * hw + gw])
    v = _gelu(proj[:, 4 * hw + gw:4 * hw + 2 * gw])
    mu = jnp.mean(v, axis=-1, keepdims=True)
    vc = v - mu
    var = jnp.mean(vc * vc, axis=-1, keepdims=True)
    vln = (vc * lax.rsqrt(var + NORM_EPS)) * lng_ref[...] + lnb_ref[...]
    tri = (lax.broadcasted_iota(I32, (GMLP_CHUNK, GMLP_CHUNK), 1)
           <= lax.broadcasted_iota(I32, (GMLP_CHUNK, GMLP_CHUNK), 0))
    gm_groups = []
    for g in range(n_groups):
        gs = slice(g * HEAD_DIM, (g + 1) * HEAD_DIM)
        wm = jnp.where(tri, sw_ref[g], 0.0).astype(BF16)
        bias = sbt_ref[:, g:g + 1]
        parts = []
        for cidx in range(tb // GMLP_CHUNK):
            rs = slice(cidx * GMLP_CHUNK, (cidx + 1) * GMLP_CHUNK)
            mixed = jnp.dot(wm, vln[rs, gs].astype(BF16), preferred_element_type=F32) + bias
            parts.append(u[rs, gs] * mixed)
        gmg = jnp.concatenate(parts, axis=0) if len(parts) > 1 else parts[0]
        gm_groups.append(_rms(gmg) * gng_ref[:, gs])
    gm = jnp.concatenate(gm_groups, axis=-1)

    cat = jnp.concatenate([o_all, gm], axis=-1).astype(BF16)
    mixed_out = jnp.dot(cat, wout_ref[...], preferred_element_type=F32)
    o_ref[...] = x + gate * mixed_out


def _mix_call(x, mod8, g1, w_in, lb, hg, lng, lnb, gng, sw, sbt, w_out, *, tb):
    b, s, d = x.shape
    hw = lb.shape[1]
    gw = lng.shape[1]
    n_heads = hw // HEAD_DIM
    kern = functools.partial(_mix_kernel, tb=tb, hw=hw, gw=gw)
    return pl.pallas_call(
        kern,
        grid=(b, s // tb),
        in_specs=[pl.BlockSpec((None, tb, d), lambda i, j: (i, j, 0)),
                  pl.BlockSpec((None, SUBLANES, d), lambda i, j: (i, 0, 0)),
                  _const_spec(g1.shape), _const_spec(w_in.shape), _const_spec(lb.shape),
                  _const_spec(hg.shape), _const_spec(lng.shape), _const_spec(lnb.shape),
                  _const_spec(gng.shape), _const_spec(sw.shape), _const_spec(sbt.shape),
                  _const_spec(w_out.shape)],
        out_specs=pl.BlockSpec((None, tb, d), lambda i, j: (i, j, 0)),
        out_shape=jax.ShapeDtypeStruct((b, s, d), F32),
        scratch_shapes=[pltpu.VMEM((n_heads, HEAD_DIM, HEAD_DIM), F32)]
                       + [pltpu.VMEM((tb, hw), F32) for _ in range(7)],
        compiler_params=pltpu.CompilerParams(dimension_semantics=("arbitrary", "arbitrary"),
                                             vmem_limit_bytes=VMEM_LIMIT_MIX),
        name="hgrn_gmlp_mixer",
    )(x, mod8, g1, w_in, lb, hg, lng, lnb, gng, sw, sbt, w_out)


def _topk_rows(x, k, out):
    n = x.shape[0]
    rid = lax.broadcasted_iota(I32, x.shape, 0).astype(F32)
    vals, ids = [], []
    for _ in range(k):
        m = jnp.max(x, axis=0, keepdims=True)
        am = jnp.min(jnp.where(x == m, rid, float(n)), axis=0, keepdims=True)
        vals.append(m)
        ids.append(am)
        x = jnp.where(rid == am, -jnp.inf, x)
        yield
    out.append((jnp.concatenate(vals, axis=0), jnp.concatenate(ids, axis=0)))


def _head_topk(load_sim1, load_sim2, out):
    k = PEER_TOPK
    tops = []
    yield from _topk_rows(load_sim1(), k, tops)
    yield from _topk_rows(load_sim2(), k, tops)
    (s1, i1), (s2, i2) = tops
    t = s1.shape[1]
    jrow8 = lax.broadcasted_iota(I32, (SUBLANES, t), 0)
    cs = [s1[0:1] + s2]
    ci = [i1[0:1] * N_KEYS + i2]
    for i in range(1, SUBLANES):
        nvalid = k // (i + 1)
        cs.append(jnp.where(jrow8 < nvalid, s1[i:i + 1] + s2[0:SUBLANES], -jnp.inf))
        ci.append(i1[i:i + 1] * N_KEYS + i2[0:SUBLANES])
    cs.append(s1[SUBLANES:k] + s2[0:1])
    ci.append(i1[SUBLANES:k] * N_KEYS + i2[0:1])
    cand_s = jnp.concatenate(cs, axis=0)
    cand_i = jnp.concatenate(ci, axis=0)
    best = []
    yield from _topk_rows(cand_s, k, best)
    top_s, pos = best[0]
    prow = lax.broadcasted_iota(I32, cand_s.shape, 0).astype(F32)
    sel = [jnp.sum(jnp.where(prow == pos[r:r + 1], cand_i, 0.0), axis=0, keepdims=True) for r in range(k)]
    ex = jnp.exp(top_s - top_s[0:1])
    out.append((jnp.concatenate(sel, axis=0), ex / jnp.sum(ex, axis=0, keepdims=True)))


def _sim_kernel(x_ref, mod_ref, g2_ref, wq_ref, keys_ref, h_ref, sim_ref, *, n_heads):
    x = x_ref[...]
    shift, scale = (mod_ref[r:r + 1, :] for r in (MOD_SHIFT2, MOD_SCALE2))
    h = (_rms(x) * g2_ref[...]) * (1.0 + scale) + shift
    tb = x.shape[0]
    for c in range(SUBLANES):
        h_ref[pl.ds(c, tb, stride=SUBLANES), :] = h[:, c * LANES:(c + 1) * LANES]
    q = jnp.dot(h.astype(BF16), wq_ref[...], preferred_element_type=F32)
    for hp in range(2 * n_heads):
        sim_ref[hp] = lax.dot_general(keys_ref[hp // 2, hp % 2].astype(BF16),
                                      q[:, hp * HEAD_DIM:(hp + 1) * HEAD_DIM].astype(BF16),
                                      (((1,), (1,)), ((), ())), preferred_element_type=F32)


def _sim_call(x1, mod8, g2, wq, keys):
    b, s, d = x1.shape
    n_heads = keys.shape[0]
    tb = SIM_TILE
    nblk = s // tb
    kern = functools.partial(_sim_kernel, n_heads=n_heads)
    return pl.pallas_call(
        kern,
        grid=(b, nblk),
        in_specs=[pl.BlockSpec((None, tb, d), lambda i, j: (i, j, 0)),
                  pl.BlockSpec((None, SUBLANES, d), lambda i, j: (i, 0, 0)),
                  _const_spec(g2.shape), _const_spec(wq.shape), _const_spec(keys.shape)],
        out_specs=[pl.BlockSpec((tb * SUBLANES, LANES), lambda i, j: (i * nblk + j, 0)),
                   pl.BlockSpec((None, 2 * n_heads, N_KEYS, tb), lambda i, j: (i * nblk + j, 0, 0, 0))],
        out_shape=[jax.ShapeDtypeStruct((b * s * SUBLANES, LANES), F32),
                   jax.ShapeDtypeStruct((b * nblk, 2 * n_heads, N_KEYS, tb), F32)],
        compiler_params=pltpu.CompilerParams(dimension_semantics=("arbitrary", "arbitrary"),
                                             vmem_limit_bytes=VMEM_LIMIT_MIX),
        name="peer_similarities",
    )(x1, mod8, g2, wq, keys)


def _pack_kernel(x_ref, o_ref, stage):
    r = x_ref.shape[0]
    for c in range(SUBLANES):
        stage[pl.ds(c, r, stride=SUBLANES), :] = x_ref[:, c * LANES:(c + 1) * LANES]
    o_ref[...] = pltpu.bitcast(stage[...].astype(BF16), I32)


def _pack_table(tab, *, rows=512):
    n, d = tab.shape
    assert d == SUBLANES * LANES and n % rows == 0
    return pl.pallas_call(
        _pack_kernel,
        grid=(n // rows,),
        in_specs=[pl.BlockSpec((rows, d), lambda i: (i, 0))],
        out_specs=pl.BlockSpec((rows * ROW_WORDS, LANES), lambda i: (i, 0)),
        out_shape=jax.ShapeDtypeStruct((n * ROW_WORDS, LANES), I32),
        scratch_shapes=[pltpu.VMEM((rows * SUBLANES, LANES), F32)],
        name="peer_table_pack",
    )(tab)


def _gather_tile(tab_ref, ibuf, slot, positions):
    pieces = [tab_ref[pl.ds(pl.multiple_of(ibuf[slot, u, k], ROW_WORDS), ROW_WORDS), :]
              for u, k in positions]
    return pltpu.bitcast(jnp.concatenate(pieces, axis=0), BF16)


def _stage_ids(idx_ref, ibuf, sem, first_token, slot, group):
    return pltpu.make_async_copy(idx_ref.at[pl.ds(first_token, group)], ibuf.at[slot], sem.at[slot])


def _staged_token_loop(idx_ref, ibuf, sem, tbp, group, open_trip):
    ngroups = tbp // group

    def stage(g, slot):
        return _stage_ids(idx_ref, ibuf, sem, pl.multiple_of(g * group, group), slot, group)

    for s in range(IDX_SLOTS - 1):
        stage(s, s).start()

    def trip(i, carry):
        per_token, per_group = open_trip(i)
        for s in range(IDX_SLOTS):
            g = i * IDX_SLOTS + s
            stage(g, s).wait()
            stage(jnp.minimum(g + IDX_SLOTS - 1, ngroups - 1), (s + IDX_SLOTS - 1) % IDX_SLOTS).start()
            side = iter(()) if per_group is None else per_group(s)
            for u in range(group):
                per_token(s * group + u, s, u)
                for _ in range(SIDE_STEPS_PER_TOKEN):
                    next(side, None)
            for _ in side:
                pass
        return carry

    lax.fori_loop(0, ngroups // IDX_SLOTS, trip, 0)
    for s in range(IDX_SLOTS - 1):
        stage(ngroups - 1, s).wait()


def _window(ref, first, n):
    return ref.at[pl.ds(pl.multiple_of(first, SUBLANES), n)]


def _split_bf16(x, parts):
    out = []
    for _ in range(parts - 1):
        p = x.astype(BF16)
        out.append(p)
        x = x - p.astype(F32)
    out.append(x.astype(BF16))
    return out


def _sel_order(q):
    return (q % SUBLANES) * PEER_TOPK + q // SUBLANES


def _peer_u_kernel(sim_ref, h_ref, tab_ref, bsel_ref, act_ref, idx_ref, ibuf, sem, f_s, idt_s, gt_s, ids_s, gate_s,
                   *, tbp, nsel):
    trip = U_GROUP * IDX_SLOTS
    n_heads = nsel // PEER_TOPK
    units_per_group = (tbp // SIM_TILE) * n_heads * U_GROUP // tbp
    assert units_per_group * (tbp // U_GROUP) == (tbp // SIM_TILE) * n_heads

    @pl.when(pl.program_id(0) == 0)
    def _():
        ids_s[...] = jnp.zeros_like(ids_s)
        gate_s[...] = jnp.zeros_like(gate_s)

    idx_ref[...] = ids_s[...]

    def route_unit(w):
        tile, hh = w // n_heads, w % n_heads
        out = []
        yield from _head_topk(lambda: sim_ref[tile, 2 * hh], lambda: sim_ref[tile, 2 * hh + 1], out)
        ids, gates = out[0]
        idt_s[tile, pl.ds(pl.multiple_of(hh * PEER_TOPK, PEER_TOPK), PEER_TOPK), :] = ids * float(ROW_WORDS)
        for r in range(PEER_TOPK):
            gt_s[tile, pl.ds(r * n_heads + hh, 1), :] = gates[r:r + 1]

    sub = lax.broadcasted_iota(I32, (SUBLANES, LANES), 0)
    lane = lax.broadcasted_iota(I32, (SUBLANES, LANES), 1)
    diag = sub == (lane & (SUBLANES - 1))
    m4 = (sub & 4) == 0
    m2 = (sub & 2) == 0
    m1 = (sub & 1) == 0

    def fold(a, b, dist, mask):
        return jnp.where(mask, a + pltpu.roll(a, SUBLANES - dist, 0), b + pltpu.roll(b, dist, 0))

    def column_sums(p):
        a0, a1 = fold(p[0], p[4], 4, m4), fold(p[2], p[6], 4, m4)
        a2, a3 = fold(p[1], p[5], 4, m4), fold(p[3], p[7], 4, m4)
        return fold(fold(a0, a1, 2, m2), fold(a2, a3, 2, m2), 1, m1)

    def partial_scores(h_win, tt, slot, u):
        hv = h_win[tt * SUBLANES:(tt + 1) * SUBLANES, :]
        h16 = jnp.concatenate(_split_bf16(hv, 2), axis=0)
        z = []
        for j in range(nsel // SEL_TILE):
            m = _gather_tile(tab_ref, ibuf, slot, [(u, j * SEL_TILE + i) for i in range(SEL_TILE)])
            o = lax.dot_general(h16, m, (((1,), (1,)), ((), ())), preferred_element_type=F32)
            o = o[0:SUBLANES] + o[SUBLANES:2 * SUBLANES]
            z += [jnp.where(diag, o[:, c * LANES:(c + 1) * LANES], 0.0) for c in range(SEL_TILE * SUBLANES // LANES)]
        f_s[tt * SUBLANES:(tt + 1) * SUBLANES, :] = column_sums(z)

    def finish_trip(act_win, gate_win):
        acc = None
        for v in range(SUBLANES):
            xv = f_s[pl.ds(v, trip, stride=SUBLANES), :]
            d = jnp.dot(jnp.concatenate(_split_bf16(xv, 3), axis=0), bsel_ref[v], preferred_element_type=F32)
            acc = d if acc is None else acc + d
        s = acc[0:trip] + acc[trip:2 * trip] + acc[2 * trip:3 * trip]
        act_win[...] = _gelu(s) * gate_win[...]

    f_s[...] = jnp.zeros_like(f_s)

    def open_trip(i):
        tok0 = i * trip
        prev0 = jnp.maximum(tok0 - trip, 0)
        finish_trip(_window(act_ref, prev0, trip), _window(gate_s, prev0, trip))
        h_win = _window(h_ref, tok0 * SUBLANES, trip * SUBLANES)

        def per_group(s):
            for n in range(units_per_group):
                yield from route_unit((i * IDX_SLOTS + s) * units_per_group + n)

        return functools.partial(partial_scores, h_win), per_group

    _staged_token_loop(ids_s, ibuf, sem, tbp, U_GROUP, open_trip)
    finish_trip(_window(act_ref, tbp - trip, trip), _window(gate_s, tbp - trip, trip))
    for tile in range(tbp // SIM_TILE):
        tok = slice(tile * SIM_TILE, (tile + 1) * SIM_TILE)
        ids_s[tok, :] = idt_s[tile].T.astype(I32)
        gate_s[tok, :] = gt_s[tile].T


def _peer_v_kernel(idx_ref, act_ref, tab_ref, rep_ref, y_ref, ibuf, sem, arep_s, *, tbp, nsel):
    trip = V_GROUP * IDX_SLOTS
    width = nsel * SUBLANES
    diag = (lax.broadcasted_iota(I32, (SUBLANES, width), 0)
            == (lax.broadcasted_iota(I32, (SUBLANES, width), 1) & (SUBLANES - 1)))

    def open_trip(i):
        tok0 = i * trip
        parts = jnp.concatenate(_split_bf16(act_ref[pl.ds(pl.multiple_of(tok0, trip), trip), :], ACT_PARTS), axis=0)
        arep_s[...] = jnp.dot(parts, rep_ref[...], preferred_element_type=F32)
        y_win = _window(y_ref, tok0 * SUBLANES, trip * SUBLANES)

        def per_token(tt, slot, u):
            lhs = jnp.concatenate(
                [jnp.where(diag, jnp.broadcast_to(arep_s[p * trip + tt:p * trip + tt + 1, :], (SUBLANES, width)), 0.0)
                 for p in range(ACT_PARTS)], axis=0).astype(BF16)
            m = _gather_tile(tab_ref, ibuf, slot, [(u, _sel_order(q)) for q in range(nsel)])
            acc = jnp.dot(lhs, m, preferred_element_type=F32)
            y = acc[0:SUBLANES]
            for p in range(1, ACT_PARTS):
                y = y + acc[p * SUBLANES:(p + 1) * SUBLANES]
            y_win[tt * SUBLANES:(tt + 1) * SUBLANES, :] = y

        return per_token, None

    _staged_token_loop(idx_ref, ibuf, sem, tbp, V_GROUP, open_trip)


def _table_spec(shape):
    return pl.BlockSpec(shape, lambda i: (0, 0), pipeline_mode=pl.Buffered(1))


def _stage_scratch(nsel, group):
    return [pltpu.SMEM((IDX_SLOTS, group, nsel), I32), pltpu.SemaphoreType.DMA((IDX_SLOTS,))]


def _peer_u_call(sims, h8, tab, *, tbp):
    t = h8.shape[0] // SUBLANES
    nsel = (sims.shape[1] // 2) * PEER_TOPK
    nblk = t // tbp
    tiles = tbp // SIM_TILE
    assert sims.shape[3] == SIM_TILE and tbp % SIM_TILE == 0

    def routed(i):
        return jnp.minimum(i, nblk - 1)

    def scored(i):
        return jnp.maximum(i - 1, 0)

    kern = functools.partial(_peer_u_kernel, tbp=tbp, nsel=nsel)
    lane = jnp.arange(LANES)
    bsel = (lane[None, None, :] == (SUBLANES * (lane // SUBLANES))[None, :, None]
            + jnp.arange(SUBLANES)[:, None, None]).astype(BF16)
    return pl.pallas_call(
        kern,
        grid=(nblk + 1,),
        in_specs=[pl.BlockSpec((tiles,) + sims.shape[1:], lambda i: (routed(i), 0, 0, 0)),
                  pl.BlockSpec((tbp * SUBLANES, LANES), lambda i: (scored(i), 0)),
                  _table_spec(tab.shape),
                  pl.BlockSpec(bsel.shape, lambda i: (0, 0, 0))],
        out_specs=[pl.BlockSpec((tbp, nsel), lambda i: (scored(i), 0)),
                   pl.BlockSpec((tbp, nsel), lambda i: (scored(i), 0))],
        out_shape=[jax.ShapeDtypeStruct((t, nsel), F32), jax.ShapeDtypeStruct((t, nsel), I32)],
        scratch_shapes=_stage_scratch(nsel, U_GROUP) + [
            pltpu.VMEM((U_GROUP * IDX_SLOTS * SUBLANES, LANES), F32),
            pltpu.VMEM((tiles, nsel, SIM_TILE), F32), pltpu.VMEM((tiles, nsel, SIM_TILE), F32),
            pltpu.VMEM((tbp, nsel), I32), pltpu.VMEM((tbp, nsel), F32)],
        compiler_params=pltpu.CompilerParams(dimension_semantics=("arbitrary",),
                                             vmem_limit_bytes=VMEM_LIMIT_PEER),
        name="peer_route_scores",
    )(sims, h8, tab, bsel)


def _peer_v_call(idx4, act, tab, *, tbp):
    t, nsel = act.shape
    kern = functools.partial(_peer_v_kernel, tbp=tbp, nsel=nsel)
    rep = jnp.repeat(jnp.eye(nsel, dtype=BF16), SUBLANES, axis=1)
    return pl.pallas_call(
        kern,
        grid=(t // tbp,),
        in_specs=[pl.BlockSpec((tbp, nsel), lambda i: (i, 0)),
                  pl.BlockSpec((tbp, nsel), lambda i: (i, 0)),
                  _table_spec(tab.shape),
                  pl.BlockSpec(rep.shape, lambda i: (0, 0))],
        out_specs=pl.BlockSpec((tbp * SUBLANES, LANES), lambda i: (i, 0)),
        out_shape=jax.ShapeDtypeStruct((t * SUBLANES, LANES), F32),
        scratch_shapes=_stage_scratch(nsel, V_GROUP)
                       + [pltpu.VMEM((ACT_PARTS * V_GROUP * IDX_SLOTS, nsel * SUBLANES), F32)],
        compiler_params=pltpu.CompilerParams(dimension_semantics=("arbitrary",),
                                             vmem_limit_bytes=VMEM_LIMIT_PEER),
        name="peer_expert_mix",
    )(idx4, act, tab, rep)


def _resid_kernel(x_ref, y_ref, mod_ref, g_ref, o_ref, *, final):
    tb = x_ref.shape[0]
    y = jnp.concatenate([y_ref[pl.ds(c, tb, stride=SUBLANES), :] for c in range(SUBLANES)], axis=-1)
    z = x_ref[...] + mod_ref[MOD_GATE2:MOD_GATE2 + 1, :] * y
    o_ref[...] = _rms(z) * g_ref[...] if final else z


def _resid_call(x1, y, mod8, g, *, tb, final):
    b, s, d = x1.shape
    nblk = s // tb
    return pl.pallas_call(
        functools.partial(_resid_kernel, final=final),
        grid=(b, nblk),
        in_specs=[pl.BlockSpec((None, tb, d), lambda i, j: (i, j, 0)),
                  pl.BlockSpec((tb * SUBLANES, LANES), lambda i, j: (i * nblk + j, 0)),
                  pl.BlockSpec((None, SUBLANES, d), lambda i, j: (i, 0, 0)),
                  _const_spec(g.shape)],
        out_specs=pl.BlockSpec((None, tb, d), lambda i, j: (i, j, 0)),
        out_shape=jax.ShapeDtypeStruct((b, s, d), F32),
        name="peer_residual_norm",
    )(x1, y, mod8, g)


def _pick_block(n, pref):
    while n % pref:
        pref //= 2
    return pref


def kernel(x, c, ada_w, ada_b, norm1_g, w_in, lb_gamma, hgrn_norm_g, gmlp_ln_g, gmlp_ln_b, spatial_w,
           spatial_b, gmlp_norm_g, w_out, norm2_g, peer_wq, peer_keys, peer_u, peer_v, final_g):
    b, s, d = x.shape
    depth = ada_w.shape[0]
    t = b * s
    lower_bounds = jnp.cumsum(jax.nn.softmax(lb_gamma.astype(F32), axis=0), axis=0)
    assert b <= SUBLANES
    c_pad = jnp.zeros((SUBLANES, d), F32).at[:b].set(c)
    tb_mix = _pick_block(s, 256)
    tb_res = _pick_block(s, 512)
    tbp = _pick_block(t, 512)
    assert tbp % (max(U_GROUP, V_GROUP) * IDX_SLOTS) == 0
    for l in range(depth):
        mod = _mod_call(c_pad, ada_w[l], ada_b[l])[:b]
        mod8 = jnp.concatenate([mod.reshape(b, 6, d), jnp.zeros((b, SUBLANES - 6, d), F32)], axis=1)
        x1 = _mix_call(x, mod8, norm1_g[l][None], w_in[l].astype(BF16), lower_bounds[l][None],
                       hgrn_norm_g[l][None], gmlp_ln_g[l][None], gmlp_ln_b[l][None], gmlp_norm_g[l][None],
                       spatial_w[l], spatial_b[l].T, w_out[l].astype(BF16), tb=tb_mix)
        h2, sims = _sim_call(x1, mod8, norm2_g[l][None], peer_wq[l].astype(BF16), peer_keys[l])
        act, idx4 = _peer_u_call(sims, h2, _pack_table(peer_u[l]), tbp=tbp)
        y8 = _peer_v_call(idx4, act, _pack_table(peer_v[l]), tbp=tbp)
        last = l == depth - 1
        x = _resid_call(x1, y8, mod8, final_g[None] if last else jnp.ones((1, d), F32),
                        tb=tb_res, final=last)
    return x
```

```python
import functools

import jax
import jax.numpy as jnp
from jax import lax
from jax.experimental import pallas as pl
from jax.experimental.pallas import tpu as pltpu

F32 = jnp.float32
BF16 = jnp.bfloat16
I32 = jnp.int32
HIGHEST = lax.Precision.HIGHEST
NORM_EPS = 1e-6

LANES = 128
SUBLANES = 8
HEAD_DIM = 128
GMLP_CHUNK = 128
SUB = 16
PEER_TOPK = 16
N_KEYS = 128
ROW_WORDS = SUBLANES // 2
MOD_SHIFT1, MOD_SCALE1, MOD_GATE1, MOD_SHIFT2, MOD_SCALE2, MOD_GATE2 = range(6)
IDX_SLOTS = 2
U_GROUP = 32
V_GROUP = 64
ROUTE_STEPS = 3 * PEER_TOPK
SIM_TILE = 256
SEL_TILE = 32
ACT_PARTS = 2
VMEM_LIMIT_MIX = 48 * 1024 * 1024
VMEM_LIMIT_PEER = 52 * 1024 * 1024


def _gelu(x):
    return 0.5 * x * (1.0 + jnp.tanh(0.7978845608028654 * (x + 0.044715 * (x * x * x))))


def _sigmoid(x):
    return 1.0 / (1.0 + jnp.exp(-x))


def _rms(x, eps=NORM_EPS):
    return x * lax.rsqrt(jnp.mean(x * x, axis=-1, keepdims=True) + eps)


def _const_spec(shape):
    nd = len(shape)
    return pl.BlockSpec(shape, lambda *_: (0,) * nd)


def _mod_kernel(c_ref, w_ref, b_ref, o_ref):
    c = c_ref[...]
    ca = c * _sigmoid(c)
    o_ref[...] = jnp.dot(ca, w_ref[...], precision=HIGHEST, preferred_element_type=F32) + b_ref[...]


def _mod_call(c_pad, w, b):
    rows, d = c_pad.shape
    n = w.shape[1]
    bn = 1536 if n % 1536 == 0 else n
    return pl.pallas_call(
        _mod_kernel,
        grid=(n // bn,),
        in_specs=[pl.BlockSpec((rows, d), lambda j: (0, 0)),
                  pl.BlockSpec((d, bn), lambda j: (0, j)),
                  pl.BlockSpec((1, bn), lambda j: (0, j))],
        out_specs=pl.BlockSpec((rows, bn), lambda j: (0, j)),
        out_shape=jax.ShapeDtypeStruct((rows, n), F32),
        name="adaln_mod",
    )(c_pad, w, b.reshape(1, n))


def _mix_kernel(x_ref, mod_ref, g1_ref, win_ref, lb_ref, hg_ref, lng_ref, lnb_ref, gng_ref,
                sw_ref, sbt_ref, wout_ref, o_ref,
                state, qe_s, ke_s, qf_s, kg_s, bb_s, vv_s, cat_s, *, tb, hw, gw):
    n_heads = hw // HEAD_DIM
    n_groups = gw // HEAD_DIM

    @pl.when(pl.program_id(1) == 0)
    def _():
        state[...] = jnp.zeros_like(state)

    x = x_ref[...]
    shift, scale, gate = (mod_ref[r:r + 1, :] for r in (MOD_SHIFT1, MOD_SCALE1, MOD_GATE1))
    h = (_rms(x) * g1_ref[...]) * (1.0 + scale) + shift
    proj = jnp.dot(h.astype(BF16), win_ref[...], preferred_element_type=F32)

    lb = lb_ref[...]
    q = proj[:, 0:hw]
    fz = proj[:, hw:2 * hw]
    e = jnp.exp(-jnp.abs(fz))
    r = 1.0 / (1.0 + e)
    pos = fz >= 0.0
    sig = jnp.where(pos, r, e * r)
    sig_neg = jnp.where(pos, e * r, r)
    logf = jnp.log(lb + (1.0 - lb) * sig)
    kg = (1.0 - lb) * sig_neg
    qf = q * _sigmoid(q)
    ri = lax.broadcasted_iota(I32, (tb, tb), 0)
    ci = lax.broadcasted_iota(I32, (tb, tb), 1)
    same = (ri // SUB) == (ci // SUB)
    tri_rows = jnp.where(same & (ci <= ri), 1.0, 0.0).astype(BF16)
    tri = jnp.concatenate([tri_rows, jnp.where(same & (ci > ri), 1.0, 0.0).astype(BF16)], axis=0)
    sums = sum(jnp.dot(tri, part, preferred_element_type=F32) for part in _split_bf16(logf, 3))
    bb = sums[0:tb]
    rem = sums[tb:2 * tb]
    qe_s[...] = qf * jnp.exp(bb)
    ke_s[...] = kg * jnp.exp(rem)
    qf_s[...] = qf
    kg_s[...] = kg
    bb_s[...] = bb
    vv_s[...] = proj[:, 2 * hw:3 * hw]

    rows16 = lax.broadcasted_iota(I32, (SUB, HEAD_DIM), 0)
    rows128 = lax.broadcasted_iota(I32, (GMLP_CHUNK, HEAD_DIM), 0)
    steps_per_chunk = GMLP_CHUNK // SUB

    def chunk_body(c, carry):
        c0 = pl.multiple_of(c * GMLP_CHUNK, GMLP_CHUNK)
        for hd in range(n_heads):
            cs = slice(hd * HEAD_DIM, (hd + 1) * HEAD_DIM)
            ke_c = ke_s[pl.ds(c0, GMLP_CHUNK), cs]
            v_t = vv_s[pl.ds(c0, GMLP_CHUNK), cs].T.astype(BF16)
            st = state[hd]
            outs = []
            for j in range(steps_per_chunk):
                rs = pl.ds(c0 + j * SUB, SUB)
                bb_b, qf_b, kg_b, v_b = bb_s[rs, cs], qf_s[rs, cs], kg_s[rs, cs], vv_s[rs, cs]
                o = lax.dot_general(qe_s[rs, cs].astype(BF16), st.astype(BF16), (((1,), (1,)), ((), ())),
                                    preferred_element_type=F32)
                for s in range(SUB):
                    dec = jnp.where(rows16 >= s, jnp.exp(bb_b - bb_b[s:s + 1, :]), 0.0)
                    w = jnp.sum(qf_b * dec * kg_b[s:s + 1, :], axis=-1, keepdims=True)
                    o = o + w * v_b[s:s + 1, :]
                outs.append(o)
                in_step = (rows128 >= j * SUB) & (rows128 < (j + 1) * SUB)
                ke_m = jnp.where(in_step, ke_c, 0.0).astype(BF16)
                upd = jnp.dot(v_t, ke_m, preferred_element_type=F32)
                st = jnp.exp(bb_b[SUB - 1:SUB, :]) * st + upd
            state[hd] = st
            o_c = jnp.concatenate(outs, axis=0)
            cat_s[pl.ds(c0, GMLP_CHUNK), cs] = _rms(o_c) * hg_ref[:, cs]
        return carry

    lax.fori_loop(0, tb // GMLP_CHUNK, chunk_body, 0)

    og = proj[:, 3 * hw:4 * hw]
    o_all = cat_s[:, 0:hw] * (og * _sigmoid(og))

    u = _gelu(proj[:, 4 * hw:4 * hw + gw])
    v = _gelu(proj[:, 4 * hw + gw:4 * hw + 2 * gw])
    mu = jnp.mean(v, axis=-1, keepdims=True)
    vc = v - mu
    var = jnp.mean(vc * vc, axis=-1, keepdims=True)
    vln = (vc * lax.rsqrt(var + NORM_EPS)) * lng_ref[...] + lnb_ref[...]
    tri = (lax.broadcasted_iota(I32, (GMLP_CHUNK, GMLP_CHUNK), 1)
           <= lax.broadcasted_iota(I32, (GMLP_CHUNK, GMLP_CHUNK), 0))
    gm_groups = []
    for g in range(n_groups):
        gs = slice(g * HEAD_DIM, (g + 1) * HEAD_DIM)
        wm = jnp.where(tri, sw_ref[g], 0.0).astype(BF16)
        bias = sbt_ref[:, g:g + 1]
        parts = []
        for cidx in range(tb // GMLP_CHUNK):
            rs = slice(cidx * GMLP_CHUNK, (cidx + 1) * GMLP_CHUNK)
            mixed = jnp.dot(wm, vln[rs, gs].astype(BF16), preferred_element_type=F32) + bias
            parts.append(u[rs, gs] * mixed)
        gmg = jnp.concatenate(parts, axis=0) if len(parts) > 1 else parts[0]
        gm_groups.append(_rms(gmg) * gng_ref[:, gs])
    gm = jnp.concatenate(gm_groups, axis=-1)

    cat = jnp.concatenate([o_all, gm], axis=-1).astype(BF16)
    mixed_out = jnp.dot(cat, wout_ref[...], preferred_element_type=F32)
    o_ref[...] = x + gate * mixed_out


def _mix_call(x, mod8, g1, w_in, lb, hg, lng, lnb, gng, sw, sbt, w_out, *, tb):
    b, s, d = x.shape
    hw = lb.shape[1]
    gw = lng.shape[1]
    n_heads = hw // HEAD_DIM
    kern = functools.partial(_mix_kernel, tb=tb, hw=hw, gw=gw)
    return pl.pallas_call(
        kern,
        grid=(b, s // tb),
        in_specs=[pl.BlockSpec((None, tb, d), lambda i, j: (i, j, 0)),
                  pl.BlockSpec((None, SUBLANES, d), lambda i, j: (i, 0, 0)),
                  _const_spec(g1.shape), _const_spec(w_in.shape), _const_spec(lb.shape),
                  _const_spec(hg.shape), _const_spec(lng.shape), _const_spec(lnb.shape),
                  _const_spec(gng.shape), _const_spec(sw.shape), _const_spec(sbt.shape),
                  _const_spec(w_out.shape)],
        out_specs=pl.BlockSpec((None, tb, d), lambda i, j: (i, j, 0)),
        out_shape=jax.ShapeDtypeStruct((b, s, d), F32),
        scratch_shapes=[pltpu.VMEM((n_heads, HEAD_DIM, HEAD_DIM), F32)]
                       + [pltpu.VMEM((tb, hw), F32) for _ in range(7)],
        compiler_params=pltpu.CompilerParams(dimension_semantics=("arbitrary", "arbitrary"),
                                             vmem_limit_bytes=VMEM_LIMIT_MIX),
        name="hgrn_gmlp_mixer",
    )(x, mod8, g1, w_in, lb, hg, lng, lnb, gng, sw, sbt, w_out)


def _topk_rows(x, k, out):
    n = x.shape[0]
    rid = lax.broadcasted_iota(I32, x.shape, 0).astype(F32)
    vals, ids = [], []
    for _ in range(k):
        m = jnp.max(x, axis=0, keepdims=True)
        am = jnp.min(jnp.where(x == m, rid, float(n)), axis=0, keepdims=True)
        vals.append(m)
        ids.append(am)
        x = jnp.where(rid == am, -jnp.inf, x)
        yield
    out.append((jnp.concatenate(vals, axis=0), jnp.concatenate(ids, axis=0)))


def _head_topk(load_sim1, load_sim2, out):
    k = PEER_TOPK
    tops = []
    yield from _topk_rows(load_sim1(), k, tops)
    yield from _topk_rows(load_sim2(), k, tops)
    (s1, i1), (s2, i2) = tops
    t = s1.shape[1]
    jrow8 = lax.broadcasted_iota(I32, (SUBLANES, t), 0)
    cs = [s1[0:1] + s2]
    ci = [i1[0:1] * N_KEYS + i2]
    for i in range(1, SUBLANES):
        nvalid = k // (i + 1)
        cs.append(jnp.where(jrow8 < nvalid, s1[i:i + 1] + s2[0:SUBLANES], -jnp.inf))
        ci.append(i1[i:i + 1] * N_KEYS + i2[0:SUBLANES])
    cs.append(s1[SUBLANES:k] + s2[0:1])
    ci.append(i1[SUBLANES:k] * N_KEYS + i2[0:1])
    cand_s = jnp.concatenate(cs, axis=0)
    cand_i = jnp.concatenate(ci, axis=0)
    best = []
    yield from _topk_rows(cand_s, k, best)
    top_s, pos = best[0]
    prow = lax.broadcasted_iota(I32, cand_s.shape, 0).astype(F32)
    sel = [jnp.sum(jnp.where(prow == pos[r:r + 1], cand_i, 0.0), axis=0, keepdims=True) for r in range(k)]
    ex = jnp.exp(top_s - top_s[0:1])
    out.append((jnp.concatenate(sel, axis=0), ex / jnp.sum(ex, axis=0, keepdims=True)))


def _sim_kernel(x_ref, mod_ref, g2_ref, wq_ref, keys_ref, h_ref, sim_ref, *, n_heads):
    x = x_ref[...]
    shift, scale = (mod_ref[r:r + 1, :] for r in (MOD_SHIFT2, MOD_SCALE2))
    h = (_rms(x) * g2_ref[...]) * (1.0 + scale) + shift
    tb = x.shape[0]
    for c in range(SUBLANES):
        h_ref[pl.ds(c, tb, stride=SUBLANES), :] = h[:, c * LANES:(c + 1) * LANES]
    q = jnp.dot(h.astype(BF16), wq_ref[...], preferred_element_type=F32)
    for hp in range(2 * n_heads):
        sim_ref[hp] = lax.dot_general(keys_ref[hp // 2, hp % 2].astype(BF16),
                                      q[:, hp * HEAD_DIM:(hp + 1) * HEAD_DIM].astype(BF16),
                                      (((1,), (1,)), ((), ())), preferred_element_type=F32)


def _sim_call(x1, mod8, g2, wq, keys):
    b, s, d = x1.shape
    n_heads = keys.shape[0]
    tb = SIM_TILE
    nblk = s // tb
    kern = functools.partial(_sim_kernel, n_heads=n_heads)
    return pl.pallas_call(
        kern,
        grid=(b, nblk),
        in_specs=[pl.BlockSpec((None, tb, d), lambda i, j: (i, j, 0)),
                  pl.BlockSpec((None, SUBLANES, d), lambda i, j: (i, 0, 0)),
                  _const_spec(g2.shape), _const_spec(wq.shape), _const_spec(keys.shape)],
        out_specs=[pl.BlockSpec((tb * SUBLANES, LANES), lambda i, j: (i * nblk + j, 0)),
                   pl.BlockSpec((None, 2 * n_heads, N_KEYS, tb), lambda i, j: (i * nblk + j, 0, 0, 0))],
        out_shape=[jax.ShapeDtypeStruct((b * s * SUBLANES, LANES), F32),
                   jax.ShapeDtypeStruct((b * nblk, 2 * n_heads, N_KEYS, tb), F32)],
        compiler_params=pltpu.CompilerParams(dimension_semantics=("arbitrary", "arbitrary"),
                                             vmem_limit_bytes=VMEM_LIMIT_MIX),
        name="peer_similarities",
    )(x1, mod8, g2, wq, keys)


def _pack_kernel(x_ref, o_ref, stage):
    r = x_ref.shape[0]
    for c in range(SUBLANES):
        stage[pl.ds(c, r, stride=SUBLANES), :] = x_ref[:, c * LANES:(c + 1) * LANES]
    o_ref[...] = pltpu.bitcast(stage[...].astype(BF16), I32)


def _pack_table(tab, *, rows=512):
    n, d = tab.shape
    assert d == SUBLANES * LANES and n % rows == 0
    return pl.pallas_call(
        _pack_kernel,
        grid=(n // rows,),
        in_specs=[pl.BlockSpec((rows, d), lambda i: (i, 0))],
        out_specs=pl.BlockSpec((rows * ROW_WORDS, LANES), lambda i: (i, 0)),
        out_shape=jax.ShapeDtypeStruct((n * ROW_WORDS, LANES), I32),
        scratch_shapes=[pltpu.VMEM((rows * SUBLANES, LANES), F32)],
        name="peer_table_pack",
    )(tab)


def _gather_tile(tab_ref, ibuf, slot, positions):
    pieces = [tab_ref[pl.ds(pl.multiple_of(ibuf[slot, u, k], ROW_WORDS), ROW_WORDS), :]
              for u, k in positions]
    return pltpu.bitcast(jnp.concatenate(pieces, axis=0), BF16)


def _stage_ids(idx_ref, ibuf, sem, first_token, slot, group):
    return pltpu.make_async_copy(idx_ref.at[pl.ds(first_token, group)], ibuf.at[slot], sem.at[slot])


def _staged_token_loop(idx_ref, ibuf, sem, tbp, group, open_trip):
    ngroups = tbp // group

    def stage(g, slot):
        return _stage_ids(idx_ref, ibuf, sem, pl.multiple_of(g * group, group), slot, group)

    for s in range(IDX_SLOTS - 1):
        stage(s, s).start()

    def trip(i, carry):
        per_token, per_group, side_steps = open_trip(i)
        for s in range(IDX_SLOTS):
            g = i * IDX_SLOTS + s
            stage(g, s).wait()
            stage(jnp.minimum(g + IDX_SLOTS - 1, ngroups - 1), (s + IDX_SLOTS - 1) % IDX_SLOTS).start()
            side = iter(()) if per_group is None else per_group(s)
            for u in range(group):
                per_token(s * group + u, s, u)
                for _ in range(side_steps * (u + 1) // group - side_steps * u // group):
                    next(side, None)
            for _ in side:
                pass
        return carry

    lax.fori_loop(0, ngroups // IDX_SLOTS, trip, 0)
    for s in range(IDX_SLOTS - 1):
        stage(ngroups - 1, s).wait()


def _window(ref, first, n):
    return ref.at[pl.ds(pl.multiple_of(first, SUBLANES), n)]


def _split_bf16(x, parts):
    out = []
    for _ in range(parts - 1):
        p = x.astype(BF16)
        out.append(p)
        x = x - p.astype(F32)
    out.append(x.astype(BF16))
    return out


def _sel_order(q):
    return (q % SUBLANES) * PEER_TOPK + q // SUBLANES


def _peer_u_kernel(sim_ref, h_ref, tab_ref, bsel_ref, act_ref, idx_ref, ibuf, sem, f_s, idt_s, gt_s, ids_s, gate_s,
                   *, tbp, nsel):
    trip = U_GROUP * IDX_SLOTS
    n_heads = nsel // PEER_TOPK
    units_per_group = (tbp // SIM_TILE) * n_heads * U_GROUP // tbp
    assert units_per_group * (tbp // U_GROUP) == (tbp // SIM_TILE) * n_heads

    @pl.when(pl.program_id(0) == 0)
    def _():
        ids_s[...] = jnp.zeros_like(ids_s)
        gate_s[...] = jnp.zeros_like(gate_s)

    idx_ref[...] = ids_s[...]

    def route_unit(w):
        tile, hh = w // n_heads, w % n_heads
        out = []
        yield from _head_topk(lambda: sim_ref[tile, 2 * hh], lambda: sim_ref[tile, 2 * hh + 1], out)
        ids, gates = out[0]
        idt_s[tile, pl.ds(pl.multiple_of(hh * PEER_TOPK, PEER_TOPK), PEER_TOPK), :] = ids * float(ROW_WORDS)
        for r in range(PEER_TOPK):
            gt_s[tile, pl.ds(r * n_heads + hh, 1), :] = gates[r:r + 1]

    sub = lax.broadcasted_iota(I32, (SUBLANES, LANES), 0)
    lane = lax.broadcasted_iota(I32, (SUBLANES, LANES), 1)
    diag = sub == (lane & (SUBLANES - 1))
    m4 = (sub & 4) == 0
    m2 = (sub & 2) == 0
    m1 = (sub & 1) == 0

    def fold(a, b, dist, mask):
        return jnp.where(mask, a + pltpu.roll(a, SUBLANES - dist, 0), b + pltpu.roll(b, dist, 0))

    def column_sums(p):
        a0, a1 = fold(p[0], p[4], 4, m4), fold(p[2], p[6], 4, m4)
        a2, a3 = fold(p[1], p[5], 4, m4), fold(p[3], p[7], 4, m4)
        return fold(fold(a0, a1, 2, m2), fold(a2, a3, 2, m2), 1, m1)

    def partial_scores(h_win, tt, slot, u):
        hv = h_win[tt * SUBLANES:(tt + 1) * SUBLANES, :]
        h16 = jnp.concatenate(_split_bf16(hv, 2), axis=0)
        z = []
        for j in range(nsel // SEL_TILE):
            m = _gather_tile(tab_ref, ibuf, slot, [(u, j * SEL_TILE + i) for i in range(SEL_TILE)])
            o = lax.dot_general(h16, m, (((1,), (1,)), ((), ())), preferred_element_type=F32)
            o = o[0:SUBLANES] + o[SUBLANES:2 * SUBLANES]
            z += [jnp.where(diag, o[:, c * LANES:(c + 1) * LANES], 0.0) for c in range(SEL_TILE * SUBLANES // LANES)]
        f_s[tt * SUBLANES:(tt + 1) * SUBLANES, :] = column_sums(z)

    def finish_trip(act_win, gate_win):
        acc = None
        for v in range(SUBLANES):
            xv = f_s[pl.ds(v, trip, stride=SUBLANES), :]
            d = jnp.dot(jnp.concatenate(_split_bf16(xv, 3), axis=0), bsel_ref[v], preferred_element_type=F32)
            acc = d if acc is None else acc + d
        s = acc[0:trip] + acc[trip:2 * trip] + acc[2 * trip:3 * trip]
        act_win[...] = _gelu(s) * gate_win[...]

    f_s[...] = jnp.zeros_like(f_s)

    def open_trip(i):
        tok0 = i * trip
        prev0 = jnp.maximum(tok0 - trip, 0)
        finish_trip(_window(act_ref, prev0, trip), _window(gate_s, prev0, trip))
        h_win = _window(h_ref, tok0 * SUBLANES, trip * SUBLANES)

        def per_group(s):
            for n in range(units_per_group):
                yield from route_unit((i * IDX_SLOTS + s) * units_per_group + n)

        return functools.partial(partial_scores, h_win), per_group, units_per_group * ROUTE_STEPS

    _staged_token_loop(ids_s, ibuf, sem, tbp, U_GROUP, open_trip)
    finish_trip(_window(act_ref, tbp - trip, trip), _window(gate_s, tbp - trip, trip))
    for tile in range(tbp // SIM_TILE):
        tok = slice(tile * SIM_TILE, (tile + 1) * SIM_TILE)
        ids_s[tok, :] = idt_s[tile].T.astype(I32)
        gate_s[tok, :] = gt_s[tile].T


def _peer_v_kernel(idx_ref, act_ref, tab_ref, rep_ref, y_ref, ibuf, sem, arep_s, *, tbp, nsel):
    trip = V_GROUP * IDX_SLOTS
    width = nsel * SUBLANES
    diag = (lax.broadcasted_iota(I32, (SUBLANES, width), 0)
            == (lax.broadcasted_iota(I32, (SUBLANES, width), 1) & (SUBLANES - 1)))

    def open_trip(i):
        tok0 = i * trip
        parts = jnp.concatenate(_split_bf16(act_ref[pl.ds(pl.multiple_of(tok0, trip), trip), :], ACT_PARTS), axis=0)
        arep_s[...] = jnp.dot(parts, rep_ref[...], preferred_element_type=F32)
        y_win = _window(y_ref, tok0 * SUBLANES, trip * SUBLANES)

        def per_token(tt, slot, u):
            lhs = jnp.concatenate(
                [jnp.where(diag, jnp.broadcast_to(arep_s[p * trip + tt:p * trip + tt + 1, :], (SUBLANES, width)), 0.0)
                 for p in range(ACT_PARTS)], axis=0).astype(BF16)
            m = _gather_tile(tab_ref, ibuf, slot, [(u, _sel_order(q)) for q in range(nsel)])
            acc = jnp.dot(lhs, m, preferred_element_type=F32)
            y = acc[0:SUBLANES]
            for p in range(1, ACT_PARTS):
                y = y + acc[p * SUBLANES:(p + 1) * SUBLANES]
            y_win[tt * SUBLANES:(tt + 1) * SUBLANES, :] = y

        return per_token, None, 0

    _staged_token_loop(idx_ref, ibuf, sem, tbp, V_GROUP, open_trip)


def _table_spec(shape):
    return pl.BlockSpec(shape, lambda i: (0, 0), pipeline_mode=pl.Buffered(1))


def _stage_scratch(nsel, group):
    return [pltpu.SMEM((IDX_SLOTS, group, nsel), I32), pltpu.SemaphoreType.DMA((IDX_SLOTS,))]


def _peer_u_call(sims, h8, tab, *, tbp):
    t = h8.shape[0] // SUBLANES
    nsel = (sims.shape[1] // 2) * PEER_TOPK
    nblk = t // tbp
    tiles = tbp // SIM_TILE
    assert sims.shape[3] == SIM_TILE and tbp % SIM_TILE == 0

    def routed(i):
        return jnp.minimum(i, nblk - 1)

    def scored(i):
        return jnp.maximum(i - 1, 0)

    kern = functools.partial(_peer_u_kernel, tbp=tbp, nsel=nsel)
    lane = jnp.arange(LANES)
    bsel = (lane[None, None, :] == (SUBLANES * (lane // SUBLANES))[None, :, None]
            + jnp.arange(SUBLANES)[:, None, None]).astype(BF16)
    return pl.pallas_call(
        kern,
        grid=(nblk + 1,),
        in_specs=[pl.BlockSpec((tiles,) + sims.shape[1:], lambda i: (routed(i), 0, 0, 0)),
                  pl.BlockSpec((tbp * SUBLANES, LANES), lambda i: (scored(i), 0)),
                  _table_spec(tab.shape),
                  pl.BlockSpec(bsel.shape, lambda i: (0, 0, 0))],
        out_specs=[pl.BlockSpec((tbp, nsel), lambda i: (scored(i), 0)),
                   pl.BlockSpec((tbp, nsel), lambda i: (scored(i), 0))],
        out_shape=[jax.ShapeDtypeStruct((t, nsel), F32), jax.ShapeDtypeStruct((t, nsel), I32)],
        scratch_shapes=_stage_scratch(nsel, U_GROUP) + [
            pltpu.VMEM((U_GROUP * IDX_SLOTS * SUBLANES, LANES), F32),
            pltpu.VMEM((tiles, nsel, SIM_TILE), F32), pltpu.VMEM((tiles, nsel, SIM_TILE), F32),
            pltpu.VMEM((tbp, nsel), I32), pltpu.VMEM((tbp, nsel), F32)],
        compiler_params=pltpu.CompilerParams(dimension_semantics=("arbitrary",),
                                             vmem_limit_bytes=VMEM_LIMIT_PEER),
        name="peer_route_scores",
    )(sims, h8, tab, bsel)


def _peer_v_call(idx4, act, tab, *, tbp):
    t, nsel = act.shape
    kern = functools.partial(_peer_v_kernel, tbp=tbp, nsel=nsel)
    rep = jnp.repeat(jnp.eye(nsel, dtype=BF16), SUBLANES, axis=1)
    return pl.pallas_call(
        kern,
        grid=(t // tbp,),
        in_specs=[pl.BlockSpec((tbp, nsel), lambda i: (i, 0)),
                  pl.BlockSpec((tbp, nsel), lambda i: (i, 0)),
                  _table_spec(tab.shape),
                  pl.BlockSpec(rep.shape, lambda i: (0, 0))],
        out_specs=pl.BlockSpec((tbp * SUBLANES, LANES), lambda i: (i, 0)),
        out_shape=jax.ShapeDtypeStruct((t * SUBLANES, LANES), F32),
        scratch_shapes=_stage_scratch(nsel, V_GROUP)
                       + [pltpu.VMEM((ACT_PARTS * V_GROUP * IDX_SLOTS, nsel * SUBLANES), F32)],
        compiler_params=pltpu.CompilerParams(dimension_semantics=("arbitrary",),
                                             vmem_limit_bytes=VMEM_LIMIT_PEER),
        name="peer_expert_mix",
    )(idx4, act, tab, rep)


def _resid_kernel(x_ref, y_ref, mod_ref, g_ref, o_ref, *, final):
    tb = x_ref.shape[0]
    y = jnp.concatenate([y_ref[pl.ds(c, tb, stride=SUBLANES), :] for c in range(SUBLANES)], axis=-1)
    z = x_ref[...] + mod_ref[MOD_GATE2:MOD_GATE2 + 1, :] * y
    o_ref[...] = _rms(z) * g_ref[...] if final else z


def _resid_call(x1, y, mod8, g, *, tb, final):
    b, s, d = x1.shape
    nblk = s // tb
    return pl.pallas_call(
        functools.partial(_resid_kernel, final=final),
        grid=(b, nblk),
        in_specs=[pl.BlockSpec((None, tb, d), lambda i, j: (i, j, 0)),
                  pl.BlockSpec((tb * SUBLANES, LANES), lambda i, j: (i * nblk + j, 0)),
                  pl.BlockSpec((None, SUBLANES, d), lambda i, j: (i, 0, 0)),
                  _const_spec(g.shape)],
        out_specs=pl.BlockSpec((None, tb, d), lambda i, j: (i, j, 0)),
        out_shape=jax.ShapeDtypeStruct((b, s, d), F32),
        name="peer_residual_norm",
    )(x1, y, mod8, g)


def _pick_block(n, pref):
    while n % pref:
        pref //= 2
    return pref


def kernel(x, c, ada_w, ada_b, norm1_g, w_in, lb_gamma, hgrn_norm_g, gmlp_ln_g, gmlp_ln_b, spatial_w,
           spatial_b, gmlp_norm_g, w_out, norm2_g, peer_wq, peer_keys, peer_u, peer_v, final_g):
    b, s, d = x.shape
    depth = ada_w.shape[0]
    t = b * s
    lower_bounds = jnp.cumsum(jax.nn.softmax(lb_gamma.astype(F32), axis=0), axis=0)
    assert b <= SUBLANES
    c_pad = jnp.zeros((SUBLANES, d), F32).at[:b].set(c)
    tb_mix = _pick_block(s, 256)
    tb_res = _pick_block(s, 512)
    tbp = _pick_block(t, 512)
    assert tbp % (max(U_GROUP, V_GROUP) * IDX_SLOTS) == 0
    for l in range(depth):
        mod = _mod_call(c_pad, ada_w[l], ada_b[l])[:b]
        mod8 = jnp.concatenate([mod.reshape(b, 6, d), jnp.zeros((b, SUBLANES - 6, d), F32)], axis=1)
        x1 = _mix_call(x, mod8, norm1_g[l][None], w_in[l].astype(BF16), lower_bounds[l][None],
                       hgrn_norm_g[l][None], gmlp_ln_g[l][None], gmlp_ln_b[l][None], gmlp_norm_g[l][None],
                       spatial_w[l], spatial_b[l].T, w_out[l].astype(BF16), tb=tb_mix)
        h2, sims = _sim_call(x1, mod8, norm2_g[l][None], peer_wq[l].astype(BF16), peer_keys[l])
        act, idx4 = _peer_u_call(sims, h2, _pack_table(peer_u[l]), tbp=tbp)
        y8 = _peer_v_call(idx4, act, _pack_table(peer_v[l]), tbp=tbp)
        last = l == depth - 1
        x = _resid_call(x1, y8, mod8, final_g[None] if last else jnp.ones((1, d), F32),
                        tb=tb_res, final=last)
    return x
```

```python
import functools

import jax
import jax.numpy as jnp
from jax import lax
from jax.experimental import pallas as pl
from jax.experimental.pallas import tpu as pltpu

F32 = jnp.float32
BF16 = jnp.bfloat16
I32 = jnp.int32
HIGHEST = lax.Precision.HIGHEST
NORM_EPS = 1e-6

LANES = 128
SUBLANES = 8
HEAD_DIM = 128
GMLP_CHUNK = 128
SUB = 16
PEER_TOPK = 16
N_KEYS = 128
ROW_WORDS = SUBLANES // 2
MOD_SHIFT1, MOD_SCALE1, MOD_GATE1, MOD_SHIFT2, MOD_SCALE2, MOD_GATE2 = range(6)
IDX_SLOTS = 2
U_GROUP = 32
V_GROUP = 64
ROUTE_STEPS = 3 * PEER_TOPK
SIM_TILE = 256
SEL_TILE = 32
ACT_PARTS = 2
VMEM_LIMIT_MIX = 48 * 1024 * 1024
VMEM_LIMIT_PEER = 52 * 1024 * 1024


def _gelu(x):
    return 0.5 * x * (1.0 + jnp.tanh(0.7978845608028654 * (x + 0.044715 * (x * x * x))))


def _sigmoid(x):
    return 1.0 / (1.0 + jnp.exp(-x))


def _rms(x, eps=NORM_EPS):
    return x * lax.rsqrt(jnp.mean(x * x, axis=-1, keepdims=True) + eps)


def _const_spec(shape):
    nd = len(shape)
    return pl.BlockSpec(shape, lambda *_: (0,) * nd)


def _mod_kernel(c_ref, w_ref, b_ref, o_ref):
    c = c_ref[...]
    ca = c * _sigmoid(c)
    o_ref[...] = jnp.dot(ca, w_ref[...], precision=HIGHEST, preferred_element_type=F32) + b_ref[...]


def _mod_call(c_pad, w, b):
    rows, d = c_pad.shape
    n = w.shape[1]
    bn = 1536 if n % 1536 == 0 else n
    return pl.pallas_call(
        _mod_kernel,
        grid=(n // bn,),
        in_specs=[pl.BlockSpec((rows, d), lambda j: (0, 0)),
                  pl.BlockSpec((d, bn), lambda j: (0, j)),
                  pl.BlockSpec((1, bn), lambda j: (0, j))],
        out_specs=pl.BlockSpec((rows, bn), lambda j: (0, j)),
        out_shape=jax.ShapeDtypeStruct((rows, n), F32),
        name="adaln_mod",
    )(c_pad, w, b.reshape(1, n))


def _mix_kernel(x_ref, mod_ref, g1_ref, win_ref, lb_ref, hg_ref, lng_ref, lnb_ref, gng_ref,
                sw_ref, sbt_ref, wout_ref, o_ref,
                state, qe_s, ke_s, qf_s, kg_s, bb_s, vv_s, cat_s, *, tb, hw, gw):
    n_heads = hw // HEAD_DIM
    n_groups = gw // HEAD_DIM

    @pl.when(pl.program_id(1) == 0)
    def _():
        state[...] = jnp.zeros_like(state)

    x = x_ref[...]
    shift, scale, gate = (mod_ref[r:r + 1, :] for r in (MOD_SHIFT1, MOD_SCALE1, MOD_GATE1))
    h = (_rms(x) * g1_ref[...]) * (1.0 + scale) + shift
    proj = jnp.dot(h.astype(BF16), win_ref[...], preferred_element_type=F32)

    lb = lb_ref[...]
    q = proj[:, 0:hw]
    fz = proj[:, hw:2 * hw]
    e = jnp.exp(-jnp.abs(fz))
    r = 1.0 / (1.0 + e)
    pos = fz >= 0.0
    sig = jnp.where(pos, r, e * r)
    sig_neg = jnp.where(pos, e * r, r)
    logf = jnp.log(lb + (1.0 - lb) * sig)
    kg = (1.0 - lb) * sig_neg
    qf = q * _sigmoid(q)
    ri = lax.broadcasted_iota(I32, (tb, tb), 0)
    ci = lax.broadcasted_iota(I32, (tb, tb), 1)
    same = (ri // SUB) == (ci // SUB)
    tri_rows = jnp.where(same & (ci <= ri), 1.0, 0.0).astype(BF16)
    tri = jnp.concatenate([tri_rows, jnp.where(same & (ci > ri), 1.0, 0.0).astype(BF16)], axis=0)
    sums = sum(jnp.dot(tri, part, preferred_element_type=F32) for part in _split_bf16(logf, 3))
    bb = sums[0:tb]
    rem = sums[tb:2 * tb]
    qe_s[...] = qf * jnp.exp(bb)
    ke_s[...] = kg * jnp.exp(rem)
    qf_s[...] = qf
    kg_s[...] = kg
    bb_s[...] = bb
    vv_s[...] = proj[:, 2 * hw:3 * hw]

    rows16 = lax.broadcasted_iota(I32, (SUB, HEAD_DIM), 0)
    rows128 = lax.broadcasted_iota(I32, (GMLP_CHUNK, HEAD_DIM), 0)
    steps_per_chunk = GMLP_CHUNK // SUB

    def chunk_body(c, carry):
        c0 = pl.multiple_of(c * GMLP_CHUNK, GMLP_CHUNK)
        for hd in range(n_heads):
            cs = slice(hd * HEAD_DIM, (hd + 1) * HEAD_DIM)
            ke_c = ke_s[pl.ds(c0, GMLP_CHUNK), cs]
            v_t = vv_s[pl.ds(c0, GMLP_CHUNK), cs].T.astype(BF16)
            st = state[hd]
            outs = []
            for j in range(steps_per_chunk):
                rs = pl.ds(c0 + j * SUB, SUB)
                bb_b, qf_b, kg_b, v_b = bb_s[rs, cs], qf_s[rs, cs], kg_s[rs, cs], vv_s[rs, cs]
                o = lax.dot_general(qe_s[rs, cs].astype(BF16), st.astype(BF16), (((1,), (1,)), ((), ())),
                                    preferred_element_type=F32)
                for s in range(SUB):
                    dec = jnp.where(rows16 >= s, jnp.exp(bb_b - bb_b[s:s + 1, :]), 0.0)
                    w = jnp.sum(qf_b * dec * kg_b[s:s + 1, :], axis=-1, keepdims=True)
                    o = o + w * v_b[s:s + 1, :]
                outs.append(o)
                in_step = (rows128 >= j * SUB) & (rows128 < (j + 1) * SUB)
                ke_m = jnp.where(in_step, ke_c, 0.0).astype(BF16)
                upd = jnp.dot(v_t, ke_m, preferred_element_type=F32)
                st = jnp.exp(bb_b[SUB - 1:SUB, :]) * st + upd
            state[hd] = st
            o_c = jnp.concatenate(outs, axis=0)
            cat_s[pl.ds(c0, GMLP_CHUNK), cs] = _rms(o_c) * hg_ref[:, cs]
        return carry

    lax.fori_loop(0, tb // GMLP_CHUNK, chunk_body, 0)

    og = proj[:, 3 * hw:4 * hw]
    o_all = cat_s[:, 0:hw] * (og * _sigmoid(og))

    u = _gelu(proj[:, 4 * hw:4 * hw + gw])
    v = _gelu(proj[:, 4 * hw + gw:4 * hw + 2 * gw])
    mu = jnp.mean(v, axis=-1, keepdims=True)
    vc = v - mu
    var = jnp.mean(vc * vc, axis=-1, keepdims=True)
    vln = (vc * lax.rsqrt(var + NORM_EPS)) * lng_ref[...] + lnb_ref[...]
    tri = (lax.broadcasted_iota(I32, (GMLP_CHUNK, GMLP_CHUNK), 1)
           <= lax.broadcasted_iota(I32, (GMLP_CHUNK, GMLP_CHUNK), 0))
    gm_groups = []
    for g in range(n_groups):
        gs = slice(g * HEAD_DIM, (g + 1) * HEAD_DIM)
        wm = jnp.where(tri, sw_ref[g], 0.0).astype(BF16)
        bias = sbt_ref[:, g:g + 1]
        parts = []
        for cidx in range(tb // GMLP_CHUNK):
            rs = slice(cidx * GMLP_CHUNK, (cidx + 1) * GMLP_CHUNK)
            mixed = jnp.dot(wm, vln[rs, gs].astype(BF16), preferred_element_type=F32) + bias
            parts.append(u[rs, gs] * mixed)
        gmg = jnp.concatenate(parts, axis=0) if len(parts) > 1 else parts[0]
        gm_groups.append(_rms(gmg) * gng_ref[:, gs])
    gm = jnp.concatenate(gm_groups, axis=-1)

    cat = jnp.concatenate([o_all, gm], axis=-1).astype(BF16)
    mixed_out = jnp.dot(cat, wout_ref[...], preferred_element_type=F32)
    o_ref[...] = x + gate * mixed_out


def _mix_call(x, mod8, g1, w_in, lb, hg, lng, lnb, gng, sw, sbt, w_out, *, tb):
    b, s, d = x.shape
    hw = lb.shape[1]
    gw = lng.shape[1]
    n_heads = hw // HEAD_DIM
    kern = functools.partial(_mix_kernel, tb=tb, hw=hw, gw=gw)
    return pl.pallas_call(
        kern,
        grid=(b, s // tb),
        in_specs=[pl.BlockSpec((None, tb, d), lambda i, j: (i, j, 0)),
                  pl.BlockSpec((None, SUBLANES, d), lambda i, j: (i, 0, 0)),
                  _const_spec(g1.shape), _const_spec(w_in.shape), _const_spec(lb.shape),
                  _const_spec(hg.shape), _const_spec(lng.shape), _const_spec(lnb.shape),
                  _const_spec(gng.shape), _const_spec(sw.shape), _const_spec(sbt.shape),
                  _const_spec(w_out.shape)],
        out_specs=pl.BlockSpec((None, tb, d), lambda i, j: (i, j, 0)),
        out_shape=jax.ShapeDtypeStruct((b, s, d), F32),
        scratch_shapes=[pltpu.VMEM((n_heads, HEAD_DIM, HEAD_DIM), F32)]
                       + [pltpu.VMEM((tb, hw), F32) for _ in range(7)],
        compiler_params=pltpu.CompilerParams(dimension_semantics=("arbitrary", "arbitrary"),
                                             vmem_limit_bytes=VMEM_LIMIT_MIX),
        name="hgrn_gmlp_mixer",
    )(x, mod8, g1, w_in, lb, hg, lng, lnb, gng, sw, sbt, w_out)


def _topk_rows(x, k, out):
    n = x.shape[0]
    rid = lax.broadcasted_iota(I32, x.shape, 0).astype(F32)
    vals, ids = [], []
    for _ in range(k):
        m = jnp.max(x, axis=0, keepdims=True)
        am = jnp.min(jnp.where(x == m, rid, float(n)), axis=0, keepdims=True)
        vals.append(m)
        ids.append(am)
        x = jnp.where(rid == am, -jnp.inf, x)
        yield
    out.append((jnp.concatenate(vals, axis=0), jnp.concatenate(ids, axis=0)))


def _head_topk(load_sim1, load_sim2, out):
    k = PEER_TOPK
    tops = []
    yield from _topk_rows(load_sim1(), k, tops)
    yield from _topk_rows(load_sim2(), k, tops)
    (s1, i1), (s2, i2) = tops
    t = s1.shape[1]
    jrow8 = lax.broadcasted_iota(I32, (SUBLANES, t), 0)
    cs = [s1[0:1] + s2]
    ci = [i1[0:1] * N_KEYS + i2]
    for i in range(1, SUBLANES):
        nvalid = k // (i + 1)
        cs.append(jnp.where(jrow8 < nvalid, s1[i:i + 1] + s2[0:SUBLANES], -jnp.inf))
        ci.append(i1[i:i + 1] * N_KEYS + i2[0:SUBLANES])
    cs.append(s1[SUBLANES:k] + s2[0:1])
    ci.append(i1[SUBLANES:k] * N_KEYS + i2[0:1])
    cand_s = jnp.concatenate(cs, axis=0)
    cand_i = jnp.concatenate(ci, axis=0)
    best = []
    yield from _topk_rows(cand_s, k, best)
    top_s, pos = best[0]
    prow = lax.broadcasted_iota(I32, cand_s.shape, 0).astype(F32)
    sel = [jnp.sum(jnp.where(prow == pos[r:r + 1], cand_i, 0.0), axis=0, keepdims=True) for r in range(k)]
    ex = jnp.exp(top_s - top_s[0:1])
    out.append((jnp.concatenate(sel, axis=0), ex / jnp.sum(ex, axis=0, keepdims=True)))


def _sim_kernel(x_ref, mod_ref, g2_ref, wq_ref, keys_ref, h_ref, sim_ref, *, n_heads):
    x = x_ref[...]
    shift, scale = (mod_ref[r:r + 1, :] for r in (MOD_SHIFT2, MOD_SCALE2))
    h = (_rms(x) * g2_ref[...]) * (1.0 + scale) + shift
    tb = x.shape[0]
    for c in range(SUBLANES):
        h_ref[pl.ds(c, tb, stride=SUBLANES), :] = h[:, c * LANES:(c + 1) * LANES]
    q = jnp.dot(h.astype(BF16), wq_ref[...], preferred_element_type=F32)
    for hp in range(2 * n_heads):
        sim_ref[hp] = lax.dot_general(keys_ref[hp // 2, hp % 2].astype(BF16),
                                      q[:, hp * HEAD_DIM:(hp + 1) * HEAD_DIM].astype(BF16),
                                      (((1,), (1,)), ((), ())), preferred_element_type=F32)


def _sim_call(x1, mod8, g2, wq, keys):
    b, s, d = x1.shape
    n_heads = keys.shape[0]
    tb = SIM_TILE
    nblk = s // tb
    kern = functools.partial(_sim_kernel, n_heads=n_heads)
    return pl.pallas_call(
        kern,
        grid=(b, nblk),
        in_specs=[pl.BlockSpec((None, tb, d), lambda i, j: (i, j, 0)),
                  pl.BlockSpec((None, SUBLANES, d), lambda i, j: (i, 0, 0)),
                  _const_spec(g2.shape), _const_spec(wq.shape), _const_spec(keys.shape)],
        out_specs=[pl.BlockSpec((tb * SUBLANES, LANES), lambda i, j: (i * nblk + j, 0)),
                   pl.BlockSpec((None, 2 * n_heads, N_KEYS, tb), lambda i, j: (i * nblk + j, 0, 0, 0))],
        out_shape=[jax.ShapeDtypeStruct((b * s * SUBLANES, LANES), F32),
                   jax.ShapeDtypeStruct((b * nblk, 2 * n_heads, N_KEYS, tb), F32)],
        compiler_params=pltpu.CompilerParams(dimension_semantics=("arbitrary", "arbitrary"),
                                             vmem_limit_bytes=VMEM_LIMIT_MIX),
        name="peer_similarities",
    )(x1, mod8, g2, wq, keys)


def _pack_kernel(x_ref, o_ref, stage):
    r = x_ref.shape[0]
    for c in range(SUBLANES):
        stage[pl.ds(c, r, stride=SUBLANES), :] = x_ref[:, c * LANES:(c + 1) * LANES]
    o_ref[...] = pltpu.bitcast(stage[...].astype(BF16), I32)


def _pack_table(tab, *, rows=512):
    n, d = tab.shape
    assert d == SUBLANES * LANES and n % rows == 0
    return pl.pallas_call(
        _pack_kernel,
        grid=(n // rows,),
        in_specs=[pl.BlockSpec((rows, d), lambda i: (i, 0))],
        out_specs=pl.BlockSpec((rows * ROW_WORDS, LANES), lambda i: (i, 0)),
        out_shape=jax.ShapeDtypeStruct((n * ROW_WORDS, LANES), I32),
        scratch_shapes=[pltpu.VMEM((rows * SUBLANES, LANES), F32)],
        name="peer_table_pack",
    )(tab)


def _gather_tile(tab_ref, ibuf, slot, positions):
    pieces = [tab_ref[pl.ds(pl.multiple_of(ibuf[slot, u, k], ROW_WORDS), ROW_WORDS), :]
              for u, k in positions]
    return pltpu.bitcast(jnp.concatenate(pieces, axis=0), BF16)


def _stage_ids(idx_ref, ibuf, sem, first_token, slot, group):
    return pltpu.make_async_copy(idx_ref.at[pl.ds(first_token, group)], ibuf.at[slot], sem.at[slot])


def _staged_token_loop(idx_ref, ibuf, sem, tbp, group, open_trip):
    ngroups = tbp // group

    def stage(g, slot):
        return _stage_ids(idx_ref, ibuf, sem, pl.multiple_of(g * group, group), slot, group)

    for s in range(IDX_SLOTS - 1):
        stage(s, s).start()

    def trip(i, carry):
        per_token, per_group, side_steps = open_trip(i)
        for s in range(IDX_SLOTS):
            g = i * IDX_SLOTS + s
            stage(g, s).wait()
            stage(jnp.minimum(g + IDX_SLOTS - 1, ngroups - 1), (s + IDX_SLOTS - 1) % IDX_SLOTS).start()
            side = iter(()) if per_group is None else per_group(s)
            next(side, None)
            for u in range(group):
                per_token(s * group + u, s, u)
                for _ in range(side_steps * (u + 1) // group - side_steps * u // group):
                    next(side, None)
            for _ in side:
                pass
        return carry

    lax.fori_loop(0, ngroups // IDX_SLOTS, trip, 0)
    for s in range(IDX_SLOTS - 1):
        stage(ngroups - 1, s).wait()


def _window(ref, first, n):
    return ref.at[pl.ds(pl.multiple_of(first, SUBLANES), n)]


def _split_bf16(x, parts):
    out = []
    for _ in range(parts - 1):
        p = x.astype(BF16)
        out.append(p)
        x = x - p.astype(F32)
    out.append(x.astype(BF16))
    return out


def _sel_order(q):
    return (q % SUBLANES) * PEER_TOPK + q // SUBLANES


def _peer_u_kernel(sim_ref, h_ref, tab_ref, bsel_ref, act_ref, idx_ref, ibuf, sem, f_s, idt_s, gt_s, ids_s, gate_s,
                   *, tbp, nsel):
    trip = U_GROUP * IDX_SLOTS
    n_heads = nsel // PEER_TOPK
    units_per_group = (tbp // SIM_TILE) * n_heads * U_GROUP // tbp
    assert units_per_group * (tbp // U_GROUP) == (tbp // SIM_TILE) * n_heads

    @pl.when(pl.program_id(0) == 0)
    def _():
        ids_s[...] = jnp.zeros_like(ids_s)
        gate_s[...] = jnp.zeros_like(gate_s)

    idx_ref[...] = ids_s[...]

    def route_unit(w):
        tile, hh = w // n_heads, w % n_heads
        out = []
        yield from _head_topk(lambda: sim_ref[tile, 2 * hh], lambda: sim_ref[tile, 2 * hh + 1], out)
        ids, gates = out[0]
        idt_s[tile, pl.ds(pl.multiple_of(hh * PEER_TOPK, PEER_TOPK), PEER_TOPK), :] = ids * float(ROW_WORDS)
        for r in range(PEER_TOPK):
            gt_s[tile, pl.ds(r * n_heads + hh, 1), :] = gates[r:r + 1]

    sub = lax.broadcasted_iota(I32, (SUBLANES, LANES), 0)
    lane = lax.broadcasted_iota(I32, (SUBLANES, LANES), 1)
    diag = sub == (lane & (SUBLANES - 1))
    m4 = (sub & 4) == 0
    m2 = (sub & 2) == 0
    m1 = (sub & 1) == 0

    def fold(a, b, dist, mask):
        return jnp.where(mask, a + pltpu.roll(a, SUBLANES - dist, 0), b + pltpu.roll(b, dist, 0))

    def column_sums(p):
        a0, a1 = fold(p[0], p[4], 4, m4), fold(p[2], p[6], 4, m4)
        a2, a3 = fold(p[1], p[5], 4, m4), fold(p[3], p[7], 4, m4)
        return fold(fold(a0, a1, 2, m2), fold(a2, a3, 2, m2), 1, m1)

    def partial_scores(h_win, tt, slot, u):
        hv = h_win[tt * SUBLANES:(tt + 1) * SUBLANES, :]
        h16 = jnp.concatenate(_split_bf16(hv, 2), axis=0)
        z = []
        for j in range(nsel // SEL_TILE):
            m = _gather_tile(tab_ref, ibuf, slot, [(u, j * SEL_TILE + i) for i in range(SEL_TILE)])
            o = lax.dot_general(h16, m, (((1,), (1,)), ((), ())), preferred_element_type=F32)
            o = o[0:SUBLANES] + o[SUBLANES:2 * SUBLANES]
            z += [jnp.where(diag, o[:, c * LANES:(c + 1) * LANES], 0.0) for c in range(SEL_TILE * SUBLANES // LANES)]
        f_s[tt * SUBLANES:(tt + 1) * SUBLANES, :] = column_sums(z)

    def finish_trip(act_win, gate_win):
        acc = None
        for v in range(SUBLANES):
            xv = f_s[pl.ds(v, trip, stride=SUBLANES), :]
            d = jnp.dot(jnp.concatenate(_split_bf16(xv, 3), axis=0), bsel_ref[v], preferred_element_type=F32)
            acc = d if acc is None else acc + d
        s = acc[0:trip] + acc[trip:2 * trip] + acc[2 * trip:3 * trip]
        act_win[...] = _gelu(s) * gate_win[...]

    f_s[...] = jnp.zeros_like(f_s)

    def open_trip(i):
        tok0 = i * trip
        prev0 = jnp.maximum(tok0 - trip, 0)
        h_win = _window(h_ref, tok0 * SUBLANES, trip * SUBLANES)

        def per_group(s):
            if s == 0:
                finish_trip(_window(act_ref, prev0, trip), _window(gate_s, prev0, trip))
            yield
            for n in range(units_per_group):
                yield from route_unit((i * IDX_SLOTS + s) * units_per_group + n)

        return functools.partial(partial_scores, h_win), per_group, units_per_group * ROUTE_STEPS

    _staged_token_loop(ids_s, ibuf, sem, tbp, U_GROUP, open_trip)
    finish_trip(_window(act_ref, tbp - trip, trip), _window(gate_s, tbp - trip, trip))
    for tile in range(tbp // SIM_TILE):
        tok = slice(tile * SIM_TILE, (tile + 1) * SIM_TILE)
        ids_s[tok, :] = idt_s[tile].T.astype(I32)
        gate_s[tok, :] = gt_s[tile].T


def _peer_v_kernel(idx_ref, act_ref, tab_ref, rep_ref, y_ref, ibuf, sem, arep_s, *, tbp, nsel):
    trip = V_GROUP * IDX_SLOTS
    width = nsel * SUBLANES
    diag = (lax.broadcasted_iota(I32, (SUBLANES, width), 0)
            == (lax.broadcasted_iota(I32, (SUBLANES, width), 1) & (SUBLANES - 1)))

    def open_trip(i):
        tok0 = i * trip
        parts = jnp.concatenate(_split_bf16(act_ref[pl.ds(pl.multiple_of(tok0, trip), trip), :], ACT_PARTS), axis=0)
        arep_s[...] = jnp.dot(parts, rep_ref[...], preferred_element_type=F32)
        y_win = _window(y_ref, tok0 * SUBLANES, trip * SUBLANES)

        def per_token(tt, slot, u):
            lhs = jnp.concatenate(
                [jnp.where(diag, jnp.broadcast_to(arep_s[p * trip + tt:p * trip + tt + 1, :], (SUBLANES, width)), 0.0)
                 for p in range(ACT_PARTS)], axis=0).astype(BF16)
            m = _gather_tile(tab_ref, ibuf, slot, [(u, _sel_order(q)) for q in range(nsel)])
            acc = jnp.dot(lhs, m, preferred_element_type=F32)
            y = acc[0:SUBLANES]
            for p in range(1, ACT_PARTS):
                y = y + acc[p * SUBLANES:(p + 1) * SUBLANES]
            y_win[tt * SUBLANES:(tt + 1) * SUBLANES, :] = y

        return per_token, None, 0

    _staged_token_loop(idx_ref, ibuf, sem, tbp, V_GROUP, open_trip)


def _table_spec(shape):
    return pl.BlockSpec(shape, lambda i: (0, 0), pipeline_mode=pl.Buffered(1))


def _stage_scratch(nsel, group):
    return [pltpu.SMEM((IDX_SLOTS, group, nsel), I32), pltpu.SemaphoreType.DMA((IDX_SLOTS,))]


def _peer_u_call(sims, h8, tab, *, tbp):
    t = h8.shape[0] // SUBLANES
    nsel = (sims.shape[1] // 2) * PEER_TOPK
    nblk = t // tbp
    tiles = tbp // SIM_TILE
    assert sims.shape[3] == SIM_TILE and tbp % SIM_TILE == 0

    def routed(i):
        return jnp.minimum(i, nblk - 1)

    def scored(i):
        return jnp.maximum(i - 1, 0)

    kern = functools.partial(_peer_u_kernel, tbp=tbp, nsel=nsel)
    lane = jnp.arange(LANES)
    bsel = (lane[None, None, :] == (SUBLANES * (lane // SUBLANES))[None, :, None]
            + jnp.arange(SUBLANES)[:, None, None]).astype(BF16)
    return pl.pallas_call(
        kern,
        grid=(nblk + 1,),
        in_specs=[pl.BlockSpec((tiles,) + sims.shape[1:], lambda i: (routed(i), 0, 0, 0)),
                  pl.BlockSpec((tbp * SUBLANES, LANES), lambda i: (scored(i), 0)),
                  _table_spec(tab.shape),
                  pl.BlockSpec(bsel.shape, lambda i: (0, 0, 0))],
        out_specs=[pl.BlockSpec((tbp, nsel), lambda i: (scored(i), 0)),
                   pl.BlockSpec((tbp, nsel), lambda i: (scored(i), 0))],
        out_shape=[jax.ShapeDtypeStruct((t, nsel), F32), jax.ShapeDtypeStruct((t, nsel), I32)],
        scratch_shapes=_stage_scratch(nsel, U_GROUP) + [
            pltpu.VMEM((U_GROUP * IDX_SLOTS * SUBLANES, LANES), F32),
            pltpu.VMEM((tiles, nsel, SIM_TILE), F32), pltpu.VMEM((tiles, nsel, SIM_TILE), F32),
            pltpu.VMEM((tbp, nsel), I32), pltpu.VMEM((tbp, nsel), F32)],
        compiler_params=pltpu.CompilerParams(dimension_semantics=("arbitrary",),
                                             vmem_limit_bytes=VMEM_LIMIT_PEER),
        name="peer_route_scores",
    )(sims, h8, tab, bsel)


def _peer_v_call(idx4, act, tab, *, tbp):
    t, nsel = act.shape
    kern = functools.partial(_peer_v_kernel, tbp=tbp, nsel=nsel)
    rep = jnp.repeat(jnp.eye(nsel, dtype=BF16), SUBLANES, axis=1)
    return pl.pallas_call(
        kern,
        grid=(t // tbp,),
        in_specs=[pl.BlockSpec((tbp, nsel), lambda i: (i, 0)),
                  pl.BlockSpec((tbp, nsel), lambda i: (i, 0)),
                  _table_spec(tab.shape),
                  pl.BlockSpec(rep.shape, lambda i: (0, 0))],
        out_specs=pl.BlockSpec((tbp * SUBLANES, LANES), lambda i: (i, 0)),
        out_shape=jax.ShapeDtypeStruct((t * SUBLANES, LANES), F32),
        scratch_shapes=_stage_scratch(nsel, V_GROUP)
                       + [pltpu.VMEM((ACT_PARTS * V_GROUP * IDX_SLOTS, nsel * SUBLANES), F32)],
        compiler_params=pltpu.CompilerParams(dimension_semantics=("arbitrary",),
                                             vmem_limit_bytes=VMEM_LIMIT_PEER),
        name="peer_expert_mix",
    )(idx4, act, tab, rep)


def _resid_kernel(x_ref, y_ref, mod_ref, g_ref, o_ref, *, final):
    tb = x_ref.shape[0]
    y = jnp.concatenate([y_ref[pl.ds(c, tb, stride=SUBLANES), :] for c in range(SUBLANES)], axis=-1)
    z = x_ref[...] + mod_ref[MOD_GATE2:MOD_GATE2 + 1, :] * y
    o_ref[...] = _rms(z) * g_ref[...] if final else z


def _resid_call(x1, y, mod8, g, *, tb, final):
    b, s, d = x1.shape
    nblk = s // tb
    return pl.pallas_call(
        functools.partial(_resid_kernel, final=final),
        grid=(b, nblk),
        in_specs=[pl.BlockSpec((None, tb, d), lambda i, j: (i, j, 0)),
                  pl.BlockSpec((tb * SUBLANES, LANES), lambda i, j: (i * nblk + j, 0)),
                  pl.BlockSpec((None, SUBLANES, d), lambda i, j: (i, 0, 0)),
                  _const_spec(g.shape)],
        out_specs=pl.BlockSpec((None, tb, d), lambda i, j: (i, j, 0)),
        out_shape=jax.ShapeDtypeStruct((b, s, d), F32),
        name="peer_residual_norm",
    )(x1, y, mod8, g)


def _pick_block(n, pref):
    while n % pref:
        pref //= 2
    return pref


def kernel(x, c, ada_w, ada_b, norm1_g, w_in, lb_gamma, hgrn_norm_g, gmlp_ln_g, gmlp_ln_b, spatial_w,
           spatial_b, gmlp_norm_g, w_out, norm2_g, peer_wq, peer_keys, peer_u, peer_v, final_g):
    b, s, d = x.shape
    depth = ada_w.shape[0]
    t = b * s
    lower_bounds = jnp.cumsum(jax.nn.softmax(lb_gamma.astype(F32), axis=0), axis=0)
    assert b <= SUBLANES
    c_pad = jnp.zeros((SUBLANES, d), F32).at[:b].set(c)
    tb_mix = _pick_block(s, 256)
    tb_res = _pick_block(s, 512)
    tbp = _pick_block(t, 512)
    assert tbp % (max(U_GROUP, V_GROUP) * IDX_SLOTS) == 0
    for l in range(depth):
        mod = _mod_call(c_pad, ada_w[l], ada_b[l])[:b]
        mod8 = jnp.concatenate([mod.reshape(b, 6, d), jnp.zeros((b, SUBLANES - 6, d), F32)], axis=1)
        x1 = _mix_call(x, mod8, norm1_g[l][None], w_in[l].astype(BF16), lower_bounds[l][None],
                       hgrn_norm_g[l][None], gmlp_ln_g[l][None], gmlp_ln_b[l][None], gmlp_norm_g[l][None],
                       spatial_w[l], spatial_b[l].T, w_out[l].astype(BF16), tb=tb_mix)
        h2, sims = _sim_call(x1, mod8, norm2_g[l][None], peer_wq[l].astype(BF16), peer_keys[l])
        act, idx4 = _peer_u_call(sims, h2, _pack_table(peer_u[l]), tbp=tbp)
        y8 = _peer_v_call(idx4, act, _pack_table(peer_v[l]), tbp=tbp)
        last = l == depth - 1
        x = _resid_call(x1, y8, mod8, final_g[None] if last else jnp.ones((1, d), F32),
                        tb=tb_res, final=last)
    return x
```
